```python
import math
import jax
import jax.numpy as jnp
from jax import lax
import numpy as np

D_MODEL = 1024
BATCH = 8
SEQ = 2048
DEPTH = 4
DEC_BATCH = 128
DEC_SEQ = 1
PAST_LEN = 16384
PAGE_SIZE = 128

N_AB = (DEPTH + 1) // 2
N_C = DEPTH // 2
MIX_WIDTH = D_MODEL
RWKV_WIDTH = MIX_WIDTH // 2
RWKV_HEAD_DIM = 64
RWKV_HEADS = RWKV_WIDTH // RWKV_HEAD_DIM
DECAY_LORA = 64
AAA_LORA = 64
GATE_LORA = 128
RWKV_SPLITS = (RWKV_WIDTH, 2 * RWKV_WIDTH, 3 * RWKV_WIDTH, 3 * RWKV_WIDTH + DECAY_LORA, 3 * RWKV_WIDTH + DECAY_LORA + AAA_LORA)
RWKV_PROJ = 3 * RWKV_WIDTH + DECAY_LORA + AAA_LORA + GATE_LORA
HGRN_WIDTH = MIX_WIDTH - RWKV_WIDTH
HGRN_EXPAND = 128
HGRN_HEADS = HGRN_WIDTH // HGRN_EXPAND
HGRN_PROJ = 4 * HGRN_WIDTH
AB_PROJ = RWKV_PROJ + HGRN_PROJ
SSM_INNER = 2 * D_MODEL
SSM_HEAD_DIM = 64
SSM_HEADS = SSM_INNER // SSM_HEAD_DIM
SSM_STATE = 128
SSM_GROUPS = 4
SSM_HPG = SSM_HEADS // SSM_GROUPS
CONV_WIDTH = 4
CONV_DIM = SSM_INNER + 2 * SSM_GROUPS * SSM_STATE
SSM_PROJ = SSM_INNER + CONV_DIM + SSM_HEADS
CHUNK = 64
PEER_HEADS = 8
PEER_KEYS = 128
PEER_EXPERTS = PEER_KEYS * PEER_KEYS
PEER_KEY_DIM = 256
PEER_HALF = PEER_KEY_DIM // 2
PEER_TOPK = 16
PEER_BLOCK = 128
RMS_EPS = 1e-6
GN_EPS = 64e-5
L2_EPS = 1e-12
F32 = jnp.float32

kernel_name = 'rwkv7_hgrn2_mamba2_peer_step'


def rms_norm(x, w, eps=RMS_EPS):
    xf = x.astype(F32)
    y = xf * lax.rsqrt(jnp.mean(xf * xf, axis=-1, keepdims=True) + eps)
    return (y * w.astype(F32)).astype(x.dtype)


def to_chunks(a, c):
    b, l = a.shape[:2]
    return jnp.swapaxes(a.reshape((b, l // c, c) + a.shape[2:]), 0, 1)


def from_chunks(a):
    n, b, c = a.shape[:3]
    return jnp.swapaxes(a, 0, 1).reshape((b, n * c) + a.shape[3:])


def causal_mask(c):
    return jnp.tril(jnp.ones((c, c), dtype=bool))


def masked_decay(diff, mask):
    return jnp.where(mask, jnp.exp(jnp.where(mask, diff, 0.0)), 0.0)


def rwkv7_mixer(p, shift_prev, s0, mu, w0, w_up, a0, a_up, g_up, k_k, k_a, r_k, gn_w, gn_b):
    b, l, _ = p.shape
    p = p.astype(F32)
    prev = jnp.concatenate([shift_prev.astype(F32)[:, None], p[:, :-1]], axis=1)
    m = p + mu * (prev - p)
    r, k, v, wd, ad, gd = jnp.split(m, RWKV_SPLITS, axis=-1)
    w_raw = w0 + jnp.tanh(wd) @ w_up
    decay = jnp.exp(-jnp.exp(-jax.nn.softplus(-w_raw) - 0.5))
    a = jax.nn.sigmoid(a0 + ad @ a_up)
    g = jax.nn.sigmoid(gd) @ g_up
    heads = lambda t: t.reshape(b, l, RWKV_HEADS, RWKV_HEAD_DIM)
    kk = heads(k * k_k)
    kk = kk / jnp.maximum(jnp.sqrt(jnp.sum(kk * kk, axis=-1, keepdims=True)), L2_EPS)
    k = heads(k * (1.0 + (a - 1.0) * k_a))
    r, v, a, decay = heads(r), heads(v), heads(a), heads(decay)

    def step(s, inp):
        r_t, w_t, k_t, v_t, kk_t, a_t = inp
        sa = jnp.einsum('bhvk,bhk->bhv', s, -kk_t)
        s = (s * w_t[:, :, None, :] + sa[..., None] * (kk_t * a_t)[:, :, None, :]
             + v_t[..., None] * k_t[:, :, None, :])
        return s, jnp.einsum('bhvk,bhk->bhv', s, r_t)

    xs = tuple(jnp.swapaxes(t, 0, 1) for t in (r, decay, k, v, kk, a))
    s_fin, o = lax.scan(step, s0.astype(F32), xs)
    o = jnp.swapaxes(o, 0, 1)
    mean = jnp.mean(o, axis=-1, keepdims=True)
    var = jnp.mean(jnp.square(o - mean), axis=-1, keepdims=True)
    o = ((o - mean) * lax.rsqrt(var + GN_EPS)).reshape(b, l, RWKV_WIDTH) * gn_w + gn_b
    bonus = jnp.sum(r * k * r_k, axis=-1, keepdims=True) * v
    out = (o + bonus.reshape(b, l, RWKV_WIDTH)) * g
    return out, p[:, -1], s_fin


def chunked_gla(q, k, v, g, s0):
    c = math.gcd(q.shape[1], CHUNK)
    mask = causal_mask(c)[None, :, :, None, None]

    def step(s, inp):
        qc, kc, vc, gc = inp
        gcum = jnp.cumsum(gc, axis=1)
        rel = masked_decay(gcum[:, :, None] - gcum[:, None, :], mask)
        att = jnp.einsum('bthk,btshk->btsh', qc, rel * kc[:, None])
        o = (jnp.einsum('btsh,bshv->bthv', att, vc)
             + jnp.einsum('bthk,bhkv->bthv', qc * jnp.exp(gcum), s))
        g_end = gcum[:, -1]
        s = s * jnp.exp(g_end)[..., None] + jnp.einsum(
            'bshk,bshv->bhkv', kc * jnp.exp(g_end[:, None] - gcum), vc)
        return s, o

    s_fin, o = lax.scan(step, s0, tuple(to_chunks(t, c) for t in (q, k, v, g)))
    return from_chunks(o), s_fin


def hgrn2_mixer(p, s0, lb, norm_w):
    b, l, _ = p.shape
    p = p.astype(F32)
    q, f, i, og = jnp.split(p, 4, axis=-1)
    fg = lb + (1.0 - lb) * jax.nn.sigmoid(f)
    log_f = jnp.log(fg)
    k = 1.0 - fg
    heads = lambda t: t.reshape(b, l, HGRN_HEADS, HGRN_EXPAND)
    o, s_fin = chunked_gla(heads(jax.nn.silu(q)), heads(k), heads(i), heads(log_f), s0.astype(F32))
    o = o * lax.rsqrt(jnp.mean(o * o, axis=-1, keepdims=True) + RMS_EPS) * norm_w
    out = o.reshape(b, l, HGRN_WIDTH) * jax.nn.silu(og)
    return out, s_fin


def chunked_ssd(x, bm, cm, log_a, s0):
    c = math.gcd(x.shape[1], CHUNK)
    mask = causal_mask(c)[None, :, :, None, None]

    def step(s, inp):
        xc, bc, cc, ac = inp
        acum = jnp.cumsum(ac, axis=1)
        lmat = masked_decay(acum[:, :, None] - acum[:, None, :], mask)
        cb = jnp.einsum('btgn,bsgn->btsg', cc, bc)
        y = jnp.einsum('btsgr,bsgrp->btgrp', cb[..., None] * lmat, xc)
        y = y + jnp.einsum('btgn,bgrpn->btgrp', cc, s) * jnp.exp(acum)[..., None]
        a_end = acum[:, -1]
        s = s * jnp.exp(a_end)[..., None, None] + jnp.einsum(
            'bsgn,bsgrp->bgrpn', bc, xc * jnp.exp(a_end[:, None] - acum)[..., None])
        return s, y

    s_fin, y = lax.scan(step, s0, tuple(to_chunks(t, c) for t in (x, bm, cm, log_a)))
    return from_chunks(y), s_fin


def mamba2_mixer(p, conv_buf, s0, conv_w, conv_b, dt_bias, a_log, d_skip, norm_w):
    b, l, _ = p.shape
    p = p.astype(F32)
    z, xbc, dt = jnp.split(p, [SSM_INNER, SSM_INNER + CONV_DIM], axis=-1)
    xpad = jnp.concatenate([conv_buf.astype(F32), xbc], axis=1)
    conv = conv_b + xpad[:, 0:l] * conv_w[0]
    for j in range(1, CONV_WIDTH):
        conv = conv + xpad[:, j:j + l] * conv_w[j]
    xbc = jax.nn.silu(conv)
    new_buf = xpad[:, -(CONV_WIDTH - 1):]
    xs, bm, cm = jnp.split(xbc, [SSM_INNER, SSM_INNER + SSM_GROUPS * SSM_STATE], axis=-1)
    xs = xs.reshape(b, l, SSM_GROUPS, SSM_HPG, SSM_HEAD_DIM)
    bm = bm.reshape(b, l, SSM_GROUPS, SSM_STATE)
    cm = cm.reshape(b, l, SSM_GROUPS, SSM_STATE)
    dt = jax.nn.softplus(dt + dt_bias).reshape(b, l, SSM_GROUPS, SSM_HPG)
    log_a = dt * (-jnp.exp(a_log.astype(F32))).reshape(SSM_GROUPS, SSM_HPG)
    s0 = s0.astype(F32).reshape(b, SSM_GROUPS, SSM_HPG, SSM_HEAD_DIM, SSM_STATE)
    y, s_fin = chunked_ssd(xs * dt[..., None], bm, cm, log_a, s0)
    y = y + xs * d_skip.reshape(SSM_GROUPS, SSM_HPG, 1)
    y = y.reshape(b, l, SSM_INNER) * jax.nn.silu(z)
    yg = y.reshape(b, l, SSM_GROUPS, SSM_INNER // SSM_GROUPS)
    yg = yg * lax.rsqrt(jnp.mean(yg * yg, axis=-1, keepdims=True) + RMS_EPS)
    out = yg.reshape(b, l, SSM_INNER) * norm_w
    return out, new_buf, s_fin.reshape(b, SSM_HEADS, SSM_HEAD_DIM, SSM_STATE)


def peer_ffn(x, w_q, sub_keys, u, v):
    b, l, d = x.shape
    t = b * l
    blk = min(PEER_BLOCK, t)
    pad = (-t) % blk
    xt = jnp.pad(x.reshape(t, d), ((0, pad), (0, 0)))

    def block(xb):
        q = (xb @ w_q).reshape(blk, PEER_HEADS, 2, PEER_HALF)
        s = jnp.einsum('thcd,chkd->thck', q, sub_keys).astype(F32)
        sv, si = lax.top_k(s, PEER_TOPK)
        cand = (sv[:, :, 0, :, None] + sv[:, :, 1, None, :]).reshape(blk, PEER_HEADS, PEER_TOPK * PEER_TOPK)
        top_s, ci = lax.top_k(cand, PEER_TOPK)
        i1 = jnp.take_along_axis(si[:, :, 0], ci // PEER_TOPK, axis=-1)
        i2 = jnp.take_along_axis(si[:, :, 1], ci % PEER_TOPK, axis=-1)
        e = i1 * PEER_KEYS + i2
        gate = jax.nn.softmax(top_s, axis=-1)
        act = jax.nn.gelu(jnp.einsum('thed,td->the', u[e], xb).astype(F32), approximate=False)
        return jnp.einsum('the,thed->td', (gate * act).astype(xb.dtype), v[e])

    y = lax.map(block, xt.reshape(-1, blk, d))
    return y.reshape(-1, d)[:t].reshape(b, l, d).astype(x.dtype)


def run_trunk(x, st_shift, st_rwkv, st_hgrn, st_ssm, st_conv, w):
    lb = jax.nn.softmax(w['hgrn_lower_bounds'].astype(F32), axis=0)
    lb = jnp.cumsum(lb, axis=0) - lb[0]
    n_shift, n_rwkv, n_hgrn, n_ssm, n_conv = [], [], [], [], []
    for i in range(DEPTH):
        j = i // 2
        h = rms_norm(x, w['norm_mix'][i])
        if i % 2 == 0:
            proj = h @ w['ab_w_in'][j]
            o_a, sh, sa = rwkv7_mixer(proj[..., :RWKV_PROJ], st_shift[j], st_rwkv[j],
                                      w['rwkv_mu'][j], w['rwkv_w0'][j], w['rwkv_w_up'][j], w['rwkv_a0'][j],
                                      w['rwkv_a_up'][j], w['rwkv_g_up'][j], w['rwkv_k_k'][j], w['rwkv_k_a'][j],
                                      w['rwkv_r_k'][j], w['rwkv_gn_w'][j], w['rwkv_gn_b'][j])
            o_b, sb = hgrn2_mixer(proj[..., RWKV_PROJ:], st_hgrn[j], lb[j], w['hgrn_norm_w'][j])
            mix = jnp.concatenate([o_a, o_b], axis=-1).astype(x.dtype) @ w['ab_w_out'][j]
            n_shift.append(sh); n_rwkv.append(sa); n_hgrn.append(sb)
        else:
            proj = h @ w['ssm_w_in'][j]
            o_c, cb, sc = mamba2_mixer(proj, st_conv[j], st_ssm[j], w['ssm_conv_w'][j], w['ssm_conv_b'][j],
                                       w['ssm_dt_bias'][j], w['ssm_a_log'][j], w['ssm_d'][j], w['ssm_norm_w'][j])
            mix = o_c.astype(x.dtype) @ w['ssm_w_out'][j]
            n_conv.append(cb); n_ssm.append(sc)
        x = x + mix
        x = x + peer_ffn(rms_norm(x, w['norm_ffn'][i]), w['peer_w_q'][i], w['peer_sub_keys'][i],
                         w['peer_u'][i], w['peer_v'][i])
    y = rms_norm(x, w['norm_final'])
    dt = x.dtype
    return (y, jnp.stack(n_shift).astype(dt), jnp.stack(n_rwkv).astype(dt), jnp.stack(n_hgrn).astype(dt),
            jnp.stack(n_ssm).astype(dt), jnp.stack(n_conv).astype(dt))


def setup_inputs(seed: int = 0) -> dict:
    key = jax.random.key(seed)
    ks = jax.random.split(key, 40)
    nrm = lambda k, shape, s: s * jax.random.normal(k, shape, F32)
    ones_n = lambda k, shape: 1.0 + 0.02 * jax.random.normal(k, shape, F32)
    dt0 = jnp.exp(jax.random.uniform(ks[29], (N_C, SSM_HEADS), F32, math.log(1e-3), math.log(1e-1)))
    return {
        'x_prompt': nrm(ks[0], (BATCH, SEQ, D_MODEL), 1.0),
        'x_sample': nrm(ks[1], (DEC_BATCH, DEC_SEQ, D_MODEL), 1.0),
        'state_rwkv_shift': nrm(ks[2], (N_AB, DEC_BATCH, RWKV_PROJ), 1.0),
        'state_rwkv': nrm(ks[3], (N_AB, DEC_BATCH, RWKV_HEADS, RWKV_HEAD_DIM, RWKV_HEAD_DIM), 0.5),
        'state_hgrn': nrm(ks[4], (N_AB, DEC_BATCH, HGRN_HEADS, HGRN_EXPAND, HGRN_EXPAND), 0.5),
        'state_ssm': nrm(ks[5], (N_C, DEC_BATCH, SSM_HEADS, SSM_HEAD_DIM, SSM_STATE), 0.5),
        'state_conv': nrm(ks[6], (N_C, DEC_BATCH, CONV_WIDTH - 1, CONV_DIM), 1.0),
        'norm_mix': ones_n(ks[7], (DEPTH, D_MODEL)),
        'norm_ffn': ones_n(ks[8], (DEPTH, D_MODEL)),
        'norm_final': ones_n(ks[9], (D_MODEL,)),
        'ab_w_in': nrm(ks[10], (N_AB, D_MODEL, AB_PROJ), D_MODEL ** -0.5),
        'ab_w_out': nrm(ks[11], (N_AB, MIX_WIDTH, D_MODEL), MIX_WIDTH ** -0.5),
        'rwkv_mu': jax.random.uniform(ks[12], (N_AB, RWKV_PROJ), F32),
        'rwkv_w0': -0.5 + nrm(ks[13], (N_AB, RWKV_WIDTH), 0.5),
        'rwkv_w_up': nrm(ks[14], (N_AB, DECAY_LORA, RWKV_WIDTH), 0.5 * DECAY_LORA ** -0.5),
        'rwkv_a0': nrm(ks[15], (N_AB, RWKV_WIDTH), 0.5),
        'rwkv_a_up': nrm(ks[16], (N_AB, AAA_LORA, RWKV_WIDTH), 0.5 * AAA_LORA ** -0.5),
        'rwkv_g_up': nrm(ks[17], (N_AB, GATE_LORA, RWKV_WIDTH), GATE_LORA ** -0.5),
        'rwkv_k_k': 0.85 + nrm(ks[18], (N_AB, RWKV_WIDTH), 0.1),
        'rwkv_k_a': 1.0 + nrm(ks[19], (N_AB, RWKV_WIDTH), 0.1),
        'rwkv_r_k': nrm(ks[20], (N_AB, RWKV_HEADS, RWKV_HEAD_DIM), 0.1),
        'rwkv_gn_w': ones_n(ks[21], (N_AB, RWKV_WIDTH)),
        'rwkv_gn_b': nrm(ks[22], (N_AB, RWKV_WIDTH), 0.02),
        'hgrn_lower_bounds': nrm(ks[23], (N_AB, HGRN_WIDTH), 0.5),
        'hgrn_norm_w': ones_n(ks[24], (N_AB, HGRN_EXPAND)),
        'ssm_w_in': nrm(ks[25], (N_C, D_MODEL, SSM_PROJ), D_MODEL ** -0.5),
        'ssm_conv_w': nrm(ks[26], (N_C, CONV_WIDTH, CONV_DIM), CONV_WIDTH ** -0.5),
        'ssm_conv_b': nrm(ks[27], (N_C, CONV_DIM), 0.02),
        'ssm_dt_bias': dt0 + jnp.log(-jnp.expm1(-dt0)),
        'ssm_a_log': jnp.log(jax.random.uniform(ks[30], (N_C, SSM_HEADS), F32, 1.0, 16.0)),
        'ssm_d': 1.0 + nrm(ks[31], (N_C, SSM_HEADS), 0.1),
        'ssm_norm_w': ones_n(ks[32], (N_C, SSM_INNER)),
        'ssm_w_out': nrm(ks[33], (N_C, SSM_INNER, D_MODEL), SSM_INNER ** -0.5),
        'peer_w_q': nrm(ks[34], (DEPTH, D_MODEL, PEER_HEADS * PEER_KEY_DIM), D_MODEL ** -0.5),
        'peer_sub_keys': nrm(ks[35], (DEPTH, 2, PEER_HEADS, PEER_KEYS, PEER_HALF), PEER_HALF ** -0.5),
        'peer_u': nrm(ks[36], (DEPTH, PEER_EXPERTS, D_MODEL), D_MODEL ** -0.5),
        'peer_v': nrm(ks[37], (DEPTH, PEER_EXPERTS, D_MODEL), 0.5 * PEER_HEADS ** -0.5),
    }


def reference(x_prompt, x_sample, state_rwkv_shift, state_rwkv, state_hgrn, state_ssm, state_conv,
              norm_mix, norm_ffn, norm_final, ab_w_in, ab_w_out, rwkv_mu, rwkv_w0, rwkv_w_up, rwkv_a0,
              rwkv_a_up, rwkv_g_up, rwkv_k_k, rwkv_k_a, rwkv_r_k, rwkv_gn_w, rwkv_gn_b, hgrn_lower_bounds,
              hgrn_norm_w, ssm_w_in, ssm_conv_w, ssm_conv_b, ssm_dt_bias, ssm_a_log, ssm_d, ssm_norm_w,
              ssm_w_out, peer_w_q, peer_sub_keys, peer_u, peer_v):
    w = dict(norm_mix=norm_mix, norm_ffn=norm_ffn, norm_final=norm_final, ab_w_in=ab_w_in, ab_w_out=ab_w_out,
             rwkv_mu=rwkv_mu, rwkv_w0=rwkv_w0, rwkv_w_up=rwkv_w_up, rwkv_a0=rwkv_a0, rwkv_a_up=rwkv_a_up,
             rwkv_g_up=rwkv_g_up, rwkv_k_k=rwkv_k_k, rwkv_k_a=rwkv_k_a, rwkv_r_k=rwkv_r_k, rwkv_gn_w=rwkv_gn_w,
             rwkv_gn_b=rwkv_gn_b, hgrn_lower_bounds=hgrn_lower_bounds, hgrn_norm_w=hgrn_norm_w,
             ssm_w_in=ssm_w_in, ssm_conv_w=ssm_conv_w, ssm_conv_b=ssm_conv_b, ssm_dt_bias=ssm_dt_bias,
             ssm_a_log=ssm_a_log, ssm_d=ssm_d, ssm_norm_w=ssm_norm_w, ssm_w_out=ssm_w_out,
             peer_w_q=peer_w_q, peer_sub_keys=peer_sub_keys, peer_u=peer_u, peer_v=peer_v)
    bp = x_prompt.shape[0]
    y_prompt, p_shift, p_rwkv, p_hgrn, p_ssm, p_conv = run_trunk(
        x_prompt,
        jnp.zeros((N_AB, bp, RWKV_PROJ), F32),
        jnp.zeros((N_AB, bp, RWKV_HEADS, RWKV_HEAD_DIM, RWKV_HEAD_DIM), F32),
        jnp.zeros((N_AB, bp, HGRN_HEADS, HGRN_EXPAND, HGRN_EXPAND), F32),
        jnp.zeros((N_C, bp, SSM_HEADS, SSM_HEAD_DIM, SSM_STATE), F32),
        jnp.zeros((N_C, bp, CONV_WIDTH - 1, CONV_DIM), F32),
        w)
    y_sample, s_shift, s_rwkv, s_hgrn, s_ssm, s_conv = run_trunk(
        x_sample, state_rwkv_shift, state_rwkv, state_hgrn, state_ssm, state_conv, w)
    return (y_prompt, y_sample, p_shift, p_rwkv, p_hgrn, p_ssm, p_conv, s_shift, s_rwkv, s_hgrn, s_ssm, s_conv)
```

```python
import functools
import math

import jax
import jax.numpy as jnp
from jax import lax
from jax.experimental import pallas as pl
from jax.experimental.pallas import tpu as pltpu

D_MODEL = 1024
DEPTH = 4
MIX_WIDTH = D_MODEL
RWKV_WIDTH = MIX_WIDTH // 2
RWKV_HEAD_DIM = 64
RWKV_HEADS = RWKV_WIDTH // RWKV_HEAD_DIM
DECAY_LORA = 64
AAA_LORA = 64
GATE_LORA = 128
RWKV_SPLITS = (RWKV_WIDTH, 2 * RWKV_WIDTH, 3 * RWKV_WIDTH, 3 * RWKV_WIDTH + DECAY_LORA,
               3 * RWKV_WIDTH + DECAY_LORA + AAA_LORA)
RWKV_PROJ = 3 * RWKV_WIDTH + DECAY_LORA + AAA_LORA + GATE_LORA
HGRN_WIDTH = MIX_WIDTH - RWKV_WIDTH
HGRN_EXPAND = 128
HGRN_HEADS = HGRN_WIDTH // HGRN_EXPAND
HGRN_PROJ = 4 * HGRN_WIDTH
AB_PROJ = RWKV_PROJ + HGRN_PROJ
SSM_INNER = 2 * D_MODEL
SSM_HEAD_DIM = 64
SSM_HEADS = SSM_INNER // SSM_HEAD_DIM
SSM_STATE = 128
SSM_GROUPS = 4
SSM_HPG = SSM_HEADS // SSM_GROUPS
CONV_WIDTH = 4
CONV_DIM = SSM_INNER + 2 * SSM_GROUPS * SSM_STATE
SSM_PROJ = SSM_INNER + CONV_DIM + SSM_HEADS
CHUNK = 64
PEER_HEADS = 8
PEER_KEYS = 128
PEER_EXPERTS = PEER_KEYS * PEER_KEYS
PEER_KEY_DIM = 256
PEER_HALF = PEER_KEY_DIM // 2
PEER_TOPK = 16
RMS_EPS = 1e-6
GN_EPS = 64e-5
L2_EPS = 1e-12
F32 = jnp.float32
BF16 = jnp.bfloat16

LANES = 128
SUBLANES = 8
VMEM_LIMIT_BYTES = 56 * 1024 * 1024

PEER_I1_PER_CHUNK = 8
PEER_EXPERT_CHUNK = PEER_I1_PER_CHUNK * PEER_KEYS
PEER_ROW_TILE = 32
PEER_RANKS = PEER_TOPK + 1
PEER_RANK_ROWS = 24
MASKED = -1e30


def _pick_tile(n, candidates):
    for c in candidates:
        if n % c == 0:
            return c
    return n


def _norm_matmul_kernel(x_ref, nw_ref, w_ref, o_ref):
    x = x_ref[...]
    xn = x * lax.rsqrt(jnp.mean(x * x, axis=-1, keepdims=True) + RMS_EPS) * nw_ref[...]
    o_ref[...] = jnp.dot(xn.astype(BF16), w_ref[...], preferred_element_type=F32)


def norm_matmul(x, norm_w, w_bf16):
    m, k = x.shape
    n = w_bf16.shape[1]
    tm = _pick_tile(m, (512, 256, 128))
    tn = _pick_tile(n, (1024, 896, 768, 640, 512, 384, 256, 128))
    return pl.pallas_call(
        _norm_matmul_kernel,
        grid=(m // tm, n // tn),
        in_specs=[pl.BlockSpec((tm, k), lambda i, j: (i, 0)),
                  pl.BlockSpec((1, k), lambda i, j: (0, 0)),
                  pl.BlockSpec((k, tn), lambda i, j: (0, j))],
        out_specs=pl.BlockSpec((tm, tn), lambda i, j: (i, j)),
        out_shape=jax.ShapeDtypeStruct((m, n), F32),
        compiler_params=pltpu.CompilerParams(
            dimension_semantics=("arbitrary", "arbitrary"), vmem_limit_bytes=VMEM_LIMIT_BYTES),
        name="norm_matmul",
    )(x, norm_w.reshape(1, k), w_bf16)


def _matmul_residual_kernel(a_ref, w_ref, r_ref, o_ref):
    o_ref[...] = r_ref[...] + jnp.dot(a_ref[...].astype(BF16), w_ref[...], preferred_element_type=F32)


def matmul_residual(a, w_bf16, res):
    m, k = a.shape
    n = w_bf16.shape[1]
    tm = _pick_tile(m, (512, 256, 128))
    tn = _pick_tile(n, (1024, 512, 256, 128))
    return pl.pallas_call(
        _matmul_residual_kernel,
        grid=(m // tm, n // tn),
        in_specs=[pl.BlockSpec((tm, k), lambda i, j: (i, 0)),
                  pl.BlockSpec((k, tn), lambda i, j: (0, j)),
                  pl.BlockSpec((tm, tn), lambda i, j: (i, j))],
        out_specs=pl.BlockSpec((tm, tn), lambda i, j: (i, j)),
        out_shape=jax.ShapeDtypeStruct((m, n), F32),
        compiler_params=pltpu.CompilerParams(
            dimension_semantics=("arbitrary", "arbitrary"), vmem_limit_bytes=VMEM_LIMIT_BYTES),
        name="matmul_residual",
    )(a, w_bf16, res)


def _rms_norm_kernel(x_ref, nw_ref, o_ref):
    x = x_ref[...]
    o_ref[...] = x * lax.rsqrt(jnp.mean(x * x, axis=-1, keepdims=True) + RMS_EPS) * nw_ref[...]


def rms_norm_rows(x, norm_w):
    m, k = x.shape
    tm = _pick_tile(m, (512, 256, 128))
    return pl.pallas_call(
        _rms_norm_kernel,
        grid=(m // tm,),
        in_specs=[pl.BlockSpec((tm, k), lambda i: (i, 0)), pl.BlockSpec((1, k), lambda i: (0, 0))],
        out_specs=pl.BlockSpec((tm, k), lambda i: (i, 0)),
        out_shape=jax.ShapeDtypeStruct((m, k), F32),
        compiler_params=pltpu.CompilerParams(dimension_semantics=("arbitrary",)),
        name="rms_norm",
    )(x, norm_w.reshape(1, k))


def _top_values(s, sv_ref):
    work = s
    for k in range(PEER_RANKS):
        m = jnp.max(work, axis=0, keepdims=True)
        sv_ref[k:k + 1, :] = m
        work = jnp.where(work == m, MASKED, work)


def _peer_route(h, q_ref, keys_ref, sv1_ref, sv2_ref, cand_ref, thr1_ref, c1_ref, t2_ref, e2_ref, tb):
    q1 = q_ref[pl.ds(pl.multiple_of(h * PEER_KEY_DIM, PEER_KEY_DIM), PEER_HALF), :]
    q2 = q_ref[pl.ds(pl.multiple_of(h * PEER_KEY_DIM + PEER_HALF, PEER_HALF), PEER_HALF), :]
    s1 = jnp.dot(keys_ref[h], q1, preferred_element_type=F32)
    s2 = jnp.dot(keys_ref[PEER_HEADS + h], q2, preferred_element_type=F32)
    pad = jnp.full((PEER_RANK_ROWS - PEER_RANKS, tb), MASKED, F32)
    sv1_ref[PEER_RANKS:, :] = pad
    sv2_ref[PEER_RANKS:, :] = pad
    _top_values(s1, sv1_ref)
    _top_values(s2, sv2_ref)
    max1 = sv1_ref[0:1, :]
    max2 = sv2_ref[0:1, :]
    t1 = s1 - max1
    t2 = s2 - max2
    tv2 = jnp.where(sv2_ref[...] > 0.5 * MASKED, sv2_ref[...] - max2, MASKED)
    for a in range(PEER_RANKS):
        cand_ref[a * PEER_RANK_ROWS:(a + 1) * PEER_RANK_ROWS, :] = (sv1_ref[a:a + 1, :] - max1) + tv2
    work = cand_ref[...]
    z = jnp.zeros((1, tb), F32)
    m_prev = None
    m = None
    for k in range(PEER_RANKS):
        m_prev = m
        m = jnp.max(work, axis=0, keepdims=True)
        if k < PEER_TOPK:
            z = z + jnp.exp(m)
            work = jnp.where(work == m, MASKED, work)
    thr = 0.5 * (m_prev + m)
    thr1_ref[h] = (thr - t1).reshape(thr1_ref.shape[1:])
    c1_ref[h] = (jnp.exp(t1) / z).reshape(c1_ref.shape[1:])
    t2_ref[h] = t2
    e2_ref[h] = jnp.exp(t2)


def _peer_kernel(x_ref, nw_ref, wqt_ref, keys_ref, u_ref, vt_ref, y_ref,
                 xnt_ref, q_ref, sv1_ref, sv2_ref, cand_ref, thr1_ref, c1_ref, t2_ref, e2_ref,
                 acc_ref, a_ref, h_ref, *, tb):
    j = pl.program_id(1)

    @pl.when(j == 0)
    def _():
        x = x_ref[...]
        xn = x * lax.rsqrt(jnp.mean(x * x, axis=-1, keepdims=True) + RMS_EPS) * nw_ref[...]
        xnt_ref[...] = xn.T.astype(BF16)
        q_ref[...] = jnp.dot(wqt_ref[...], xnt_ref[...], preferred_element_type=F32).astype(BF16)
        acc_ref[...] = jnp.zeros_like(acc_ref)

        def head(h, carry):
            _peer_route(h, q_ref, keys_ref, sv1_ref, sv2_ref, cand_ref, thr1_ref, c1_ref, t2_ref, e2_ref, tb)
            return carry

        lax.fori_loop(0, PEER_HEADS, head, 0)

    a_ref[...] = jnp.dot(u_ref[...], xnt_ref[...], preferred_element_type=F32)

    for i1l in range(PEER_I1_PER_CHUNK):
        def rows(r, carry, i1l=i1l):
            r0 = pl.multiple_of(r * PEER_ROW_TILE, PEER_ROW_TILE)
            w = jnp.zeros((PEER_ROW_TILE, tb), F32)
            for h in range(PEER_HEADS):
                thr = thr1_ref[h, j, i1l:i1l + 1, :]
                c1 = c1_ref[h, j, i1l:i1l + 1, :]
                t2 = t2_ref[h, pl.ds(r0, PEER_ROW_TILE), :]
                e2 = e2_ref[h, pl.ds(r0, PEER_ROW_TILE), :]
                w = w + jnp.where(t2 >= thr, c1 * e2, 0.0)
            a = a_ref[pl.ds(i1l * PEER_KEYS + r0, PEER_ROW_TILE), :]
            g = 0.5 * a * (1.0 + lax.erf(a * (1.0 / math.sqrt(2.0))))
            h_ref[pl.ds(i1l * PEER_KEYS + r0, PEER_ROW_TILE), :] = (w * g).astype(BF16)
            return carry

        lax.fori_loop(0, PEER_KEYS // PEER_ROW_TILE, rows, 0)

    acc_ref[...] += jnp.dot(vt_ref[...], h_ref[...], preferred_element_type=F32)

    @pl.when(j == pl.num_programs(1) - 1)
    def _():
        y_ref[...] = x_ref[...] + acc_ref[...].T


def peer_block(x, norm_w, wqt_bf16, keys_bf16, u_bf16, vt_bf16):
    t, d = x.shape
    tb = _pick_tile(t, (512, 256, 128))
    n_chunks = PEER_EXPERTS // PEER_EXPERT_CHUNK
    n_groups = PEER_KEYS // PEER_I1_PER_CHUNK
    const = dict(pipeline_mode=pl.Buffered(1))
    scratch = [
        pltpu.VMEM((d, tb), BF16),
        pltpu.VMEM((PEER_HEADS * PEER_KEY_DIM, tb), BF16),
        pltpu.VMEM((PEER_RANK_ROWS, tb), F32),
        pltpu.VMEM((PEER_RANK_ROWS, tb), F32),
        pltpu.VMEM((PEER_RANKS * PEER_RANK_ROWS, tb), F32),
        pltpu.VMEM((PEER_HEADS, n_groups, PEER_I1_PER_CHUNK, tb), F32),
        pltpu.VMEM((PEER_HEADS, n_groups, PEER_I1_PER_CHUNK, tb), F32),
        pltpu.VMEM((PEER_HEADS, PEER_KEYS, tb), F32),
        pltpu.VMEM((PEER_HEADS, PEER_KEYS, tb), F32),
        pltpu.VMEM((d, tb), F32),
        pltpu.VMEM((PEER_EXPERT_CHUNK, tb), F32),
        pltpu.VMEM((PEER_EXPERT_CHUNK, tb), BF16),
    ]
    return pl.pallas_call(
        functools.partial(_peer_kernel, tb=tb),
        grid=(t // tb, n_chunks),
        in_specs=[pl.BlockSpec((tb, d), lambda i, j: (i, 0)),
                  pl.BlockSpec((1, d), lambda i, j: (0, 0)),
                  pl.BlockSpec((PEER_HEADS * PEER_KEY_DIM, d), lambda i, j: (0, 0), **const),
                  pl.BlockSpec((2 * PEER_HEADS, PEER_KEYS, PEER_HALF), lambda i, j: (0, 0, 0), **const),
                  pl.BlockSpec((PEER_EXPERT_CHUNK, d), lambda i, j: (j, 0)),
                  pl.BlockSpec((d, PEER_EXPERT_CHUNK), lambda i, j: (0, j))],
        out_specs=pl.BlockSpec((tb, d), lambda i, j: (i, 0)),
        out_shape=jax.ShapeDtypeStruct((t, d), F32),
        scratch_shapes=scratch,
        compiler_params=pltpu.CompilerParams(
            dimension_semantics=("arbitrary", "arbitrary"), vmem_limit_bytes=VMEM_LIMIT_BYTES),
        name="peer_block",
    )(x, norm_w.reshape(1, d), wqt_bf16, keys_bf16, u_bf16, vt_bf16)


def _to_chunks(a, c):
    b, l = a.shape[:2]
    return jnp.swapaxes(a.reshape((b, l // c, c) + a.shape[2:]), 0, 1)


def _from_chunks(a):
    n, b, c = a.shape[:3]
    return jnp.swapaxes(a, 0, 1).reshape((b, n * c) + a.shape[3:])


def _masked_decay(diff, mask):
    return jnp.where(mask, jnp.exp(jnp.where(mask, diff, 0.0)), 0.0)


def _rwkv7_mixer(p, shift_prev, s0, mu, w0, w_up, a0, a_up, g_up, k_k, k_a, r_k, gn_w, gn_b):
    b, l, _ = p.shape
    prev = jnp.concatenate([shift_prev[:, None], p[:, :-1]], axis=1)
    m = p + mu * (prev - p)
    r, k, v, wd, ad, gd = jnp.split(m, RWKV_SPLITS, axis=-1)
    w_raw = w0 + jnp.tanh(wd) @ w_up
    decay = jnp.exp(-jnp.exp(-jax.nn.softplus(-w_raw) - 0.5))
    a = jax.nn.sigmoid(a0 + ad @ a_up)
    g = jax.nn.sigmoid(gd) @ g_up
    heads = lambda t: t.reshape(b, l, RWKV_HEADS, RWKV_HEAD_DIM)
    kk = heads(k * k_k)
    kk = kk / jnp.maximum(jnp.sqrt(jnp.sum(kk * kk, axis=-1, keepdims=True)), L2_EPS)
    k = heads(k * (1.0 + (a - 1.0) * k_a))
    r, v, a, decay = heads(r), heads(v), heads(a), heads(decay)

    def step(s, inp):
        r_t, w_t, k_t, v_t, kk_t, a_t = inp
        sa = jnp.einsum('bhvk,bhk->bhv', s, -kk_t)
        s = (s * w_t[:, :, None, :] + sa[..., None] * (kk_t * a_t)[:, :, None, :]
             + v_t[..., None] * k_t[:, :, None, :])
        return s, jnp.einsum('bhvk,bhk->bhv', s, r_t)

    xs = tuple(jnp.swapaxes(t, 0, 1) for t in (r, decay, k, v, kk, a))
    s_fin, o = lax.scan(step, s0, xs)
    o = jnp.swapaxes(o, 0, 1)
    mean = jnp.mean(o, axis=-1, keepdims=True)
    var = jnp.mean(jnp.square(o - mean), axis=-1, keepdims=True)
    o = ((o - mean) * lax.rsqrt(var + GN_EPS)).reshape(b, l, RWKV_WIDTH) * gn_w + gn_b
    bonus = jnp.sum(r * k * r_k, axis=-1, keepdims=True) * v
    out = (o + bonus.reshape(b, l, RWKV_WIDTH)) * g
    return out, p[:, -1], s_fin


def _chunked_gla(q, k, v, g, s0):
    c = math.gcd(q.shape[1], CHUNK)
    mask = jnp.tril(jnp.ones((c, c), dtype=bool))[None, :, :, None, None]

    def step(s, inp):
        qc, kc, vc, gc = inp
        gcum = jnp.cumsum(gc, axis=1)
        rel = _masked_decay(gcum[:, :, None] - gcum[:, None, :], mask)
        att = jnp.einsum('bthk,btshk->btsh', qc, rel * kc[:, None])
        o = (jnp.einsum('btsh,bshv->bthv', att, vc)
             + jnp.einsum('bthk,bhkv->bthv', qc * jnp.exp(gcum), s))
        g_end = gcum[:, -1]
        s = s * jnp.exp(g_end)[..., None] + jnp.einsum(
            'bshk,bshv->bhkv', kc * jnp.exp(g_end[:, None] - gcum), vc)
        return s, o

    s_fin, o = lax.scan(step, s0, tuple(_to_chunks(t, c) for t in (q, k, v, g)))
    return _from_chunks(o), s_fin


def _hgrn2_mixer(p, s0, lb, norm_w):
    b, l, _ = p.shape
    q, f, i, og = jnp.split(p, 4, axis=-1)
    fg = lb + (1.0 - lb) * jax.nn.sigmoid(f)
    log_f = jnp.log(fg)
    k = 1.0 - fg
    heads = lambda t: t.reshape(b, l, HGRN_HEADS, HGRN_EXPAND)
    o, s_fin = _chunked_gla(heads(jax.nn.silu(q)), heads(k), heads(i), heads(log_f), s0)
    o = o * lax.rsqrt(jnp.mean(o * o, axis=-1, keepdims=True) + RMS_EPS) * norm_w
    out = o.reshape(b, l, HGRN_WIDTH) * jax.nn.silu(og)
    return out, s_fin


def _chunked_ssd(x, bm, cm, log_a, s0):
    c = math.gcd(x.shape[1], CHUNK)
    mask = jnp.tril(jnp.ones((c, c), dtype=bool))[None, :, :, None, None]

    def step(s, inp):
        xc, bc, cc, ac = inp
        acum = jnp.cumsum(ac, axis=1)
        lmat = _masked_decay(acum[:, :, None] - acum[:, None, :], mask)
        cb = jnp.einsum('btgn,bsgn->btsg', cc, bc)
        y = jnp.einsum('btsgr,bsgrp->btgrp', cb[..., None] * lmat, xc)
        y = y + jnp.einsum('btgn,bgrpn->btgrp', cc, s) * jnp.exp(acum)[..., None]
        a_end = acum[:, -1]
        s = s * jnp.exp(a_end)[..., None, None] + jnp.einsum(
            'bsgn,bsgrp->bgrpn', bc, xc * jnp.exp(a_end[:, None] - acum)[..., None])
        return s, y

    s_fin, y = lax.scan(step, s0, tuple(_to_chunks(t, c) for t in (x, bm, cm, log_a)))
    return _from_chunks(y), s_fin


def _mamba2_mixer(p, conv_buf, s0, conv_w, conv_b, dt_bias, a_log, d_skip, norm_w):
    b, l, _ = p.shape
    z, xbc, dt = jnp.split(p, [SSM_INNER, SSM_INNER + CONV_DIM], axis=-1)
    xpad = jnp.concatenate([conv_buf, xbc], axis=1)
    conv = conv_b + xpad[:, 0:l] * conv_w[0]
    for j in range(1, CONV_WIDTH):
        conv = conv + xpad[:, j:j + l] * conv_w[j]
    xbc = jax.nn.silu(conv)
    new_buf = xpad[:, -(CONV_WIDTH - 1):]
    xs, bm, cm = jnp.split(xbc, [SSM_INNER, SSM_INNER + SSM_GROUPS * SSM_STATE], axis=-1)
    xs = xs.reshape(b, l, SSM_GROUPS, SSM_HPG, SSM_HEAD_DIM)
    bm = bm.reshape(b, l, SSM_GROUPS, SSM_STATE)
    cm = cm.reshape(b, l, SSM_GROUPS, SSM_STATE)
    dt = jax.nn.softplus(dt + dt_bias).reshape(b, l, SSM_GROUPS, SSM_HPG)
    log_a = dt * (-jnp.exp(a_log)).reshape(SSM_GROUPS, SSM_HPG)
    s0 = s0.reshape(b, SSM_GROUPS, SSM_HPG, SSM_HEAD_DIM, SSM_STATE)
    y, s_fin = _chunked_ssd(xs * dt[..., None], bm, cm, log_a, s0)
    y = y + xs * d_skip.reshape(SSM_GROUPS, SSM_HPG, 1)
    y = y.reshape(b, l, SSM_INNER) * jax.nn.silu(z)
    yg = y.reshape(b, l, SSM_GROUPS, SSM_INNER // SSM_GROUPS)
    yg = yg * lax.rsqrt(jnp.mean(yg * yg, axis=-1, keepdims=True) + RMS_EPS)
    out = yg.reshape(b, l, SSM_INNER) * norm_w
    return out, new_buf, s_fin.reshape(b, SSM_HEADS, SSM_HEAD_DIM, SSM_STATE)


def _prepare_weights(w):
    ssm_pad = (-SSM_PROJ) % (6 * LANES)
    return dict(
        ab_w_in=w['ab_w_in'].astype(BF16),
        ab_w_out=w['ab_w_out'].astype(BF16),
        ssm_w_in=jnp.pad(w['ssm_w_in'], ((0, 0), (0, 0), (0, ssm_pad))).astype(BF16),
        ssm_w_out=w['ssm_w_out'].astype(BF16),
        peer_wqt=jnp.swapaxes(w['peer_w_q'], 1, 2).astype(BF16),
        peer_keys=w['peer_sub_keys'].reshape(DEPTH, 2 * PEER_HEADS, PEER_KEYS, PEER_HALF).astype(BF16),
        peer_u=w['peer_u'].astype(BF16),
        peer_vt=jnp.swapaxes(w['peer_v'], 1, 2).astype(BF16),
    )


def _run_trunk(x, st_shift, st_rwkv, st_hgrn, st_ssm, st_conv, w, wp):
    b, l, d = x.shape
    t = b * l
    lb = jax.nn.softmax(w['hgrn_lower_bounds'], axis=0)
    lb = jnp.cumsum(lb, axis=0) - lb[0]
    n_shift, n_rwkv, n_hgrn, n_ssm, n_conv = [], [], [], [], []
    xt = x.reshape(t, d)
    for i in range(DEPTH):
        j = i // 2
        if i % 2 == 0:
            proj = norm_matmul(xt, w['norm_mix'][i], wp['ab_w_in'][j]).reshape(b, l, AB_PROJ)
            o_a, sh, sa = _rwkv7_mixer(proj[..., :RWKV_PROJ], st_shift[j], st_rwkv[j],
                                       w['rwkv_mu'][j], w['rwkv_w0'][j], w['rwkv_w_up'][j], w['rwkv_a0'][j],
                                       w['rwkv_a_up'][j], w['rwkv_g_up'][j], w['rwkv_k_k'][j], w['rwkv_k_a'][j],
                                       w['rwkv_r_k'][j], w['rwkv_gn_w'][j], w['rwkv_gn_b'][j])
            o_b, sb = _hgrn2_mixer(proj[..., RWKV_PROJ:], st_hgrn[j], lb[j], w['hgrn_norm_w'][j])
            mix_in = jnp.concatenate([o_a, o_b], axis=-1).reshape(t, MIX_WIDTH)
            xt = matmul_residual(mix_in, wp['ab_w_out'][j], xt)
            n_shift.append(sh); n_rwkv.append(sa); n_hgrn.append(sb)
        else:
            proj = norm_matmul(xt, w['norm_mix'][i], wp['ssm_w_in'][j])[:, :SSM_PROJ].reshape(b, l, SSM_PROJ)
            o_c, cb, sc = _mamba2_mixer(proj, st_conv[j], st_ssm[j], w['ssm_conv_w'][j], w['ssm_conv_b'][j],
                                        w['ssm_dt_bias'][j], w['ssm_a_log'][j], w['ssm_d'][j], w['ssm_norm_w'][j])
            xt = matmul_residual(o_c.reshape(t, SSM_INNER), wp['ssm_w_out'][j], xt)
            n_conv.append(cb); n_ssm.append(sc)
        xt = peer_block(xt, w['norm_ffn'][i], wp['peer_wqt'][i], wp['peer_keys'][i],
                        wp['peer_u'][i], wp['peer_vt'][i])
    y = rms_norm_rows(xt, w['norm_final']).reshape(b, l, d)
    return (y, jnp.stack(n_shift), jnp.stack(n_rwkv), jnp.stack(n_hgrn), jnp.stack(n_ssm), jnp.stack(n_conv))


def kernel(x_prompt, x_sample, state_rwkv_shift, state_rwkv, state_hgrn, state_ssm, state_conv,
           norm_mix, norm_ffn, norm_final, ab_w_in, ab_w_out, rwkv_mu, rwkv_w0, rwkv_w_up, rwkv_a0,
           rwkv_a_up, rwkv_g_up, rwkv_k_k, rwkv_k_a, rwkv_r_k, rwkv_gn_w, rwkv_gn_b, hgrn_lower_bounds,
           hgrn_norm_w, ssm_w_in, ssm_conv_w, ssm_conv_b, ssm_dt_bias, ssm_a_log, ssm_d, ssm_norm_w,
           ssm_w_out, peer_w_q, peer_sub_keys, peer_u, peer_v):
    w = dict(norm_mix=norm_mix, norm_ffn=norm_ffn, norm_final=norm_final, ab_w_in=ab_w_in, ab_w_out=ab_w_out,
             rwkv_mu=rwkv_mu, rwkv_w0=rwkv_w0, rwkv_w_up=rwkv_w_up, rwkv_a0=rwkv_a0, rwkv_a_up=rwkv_a_up,
             rwkv_g_up=rwkv_g_up, rwkv_k_k=rwkv_k_k, rwkv_k_a=rwkv_k_a, rwkv_r_k=rwkv_r_k, rwkv_gn_w=rwkv_gn_w,
             rwkv_gn_b=rwkv_gn_b, hgrn_lower_bounds=hgrn_lower_bounds, hgrn_norm_w=hgrn_norm_w,
             ssm_w_in=ssm_w_in, ssm_conv_w=ssm_conv_w, ssm_conv_b=ssm_conv_b, ssm_dt_bias=ssm_dt_bias,
             ssm_a_log=ssm_a_log, ssm_d=ssm_d, ssm_norm_w=ssm_norm_w, ssm_w_out=ssm_w_out,
             peer_w_q=peer_w_q, peer_sub_keys=peer_sub_keys, peer_u=peer_u, peer_v=peer_v)
    wp = _prepare_weights(w)
    bp = x_prompt.shape[0]
    prompt = _run_trunk(
        x_prompt,
        jnp.zeros((DEPTH // 2, bp, RWKV_PROJ), F32),
        jnp.zeros((DEPTH // 2, bp, RWKV_HEADS, RWKV_HEAD_DIM, RWKV_HEAD_DIM), F32),
        jnp.zeros((DEPTH // 2, bp, HGRN_HEADS, HGRN_EXPAND, HGRN_EXPAND), F32),
        jnp.zeros((DEPTH // 2, bp, SSM_HEADS, SSM_HEAD_DIM, SSM_STATE), F32),
        jnp.zeros((DEPTH // 2, bp, CONV_WIDTH - 1, CONV_DIM), F32),
        w, wp)
    sample = _run_trunk(x_sample, state_rwkv_shift, state_rwkv, state_hgrn, state_ssm, state_conv, w, wp)
    return (prompt[0], sample[0]) + prompt[1:] + sample[1:]
```

```python
import functools
import math

import jax
import jax.numpy as jnp
from jax import lax
from jax.experimental import pallas as pl
from jax.experimental.pallas import tpu as pltpu

D_MODEL = 1024
DEPTH = 4
MIX_WIDTH = D_MODEL
RWKV_WIDTH = MIX_WIDTH // 2
RWKV_HEAD_DIM = 64
RWKV_HEADS = RWKV_WIDTH // RWKV_HEAD_DIM
DECAY_LORA = 64
AAA_LORA = 64
GATE_LORA = 128
RWKV_SPLITS = (RWKV_WIDTH, 2 * RWKV_WIDTH, 3 * RWKV_WIDTH, 3 * RWKV_WIDTH + DECAY_LORA,
               3 * RWKV_WIDTH + DECAY_LORA + AAA_LORA)
RWKV_PROJ = 3 * RWKV_WIDTH + DECAY_LORA + AAA_LORA + GATE_LORA
HGRN_WIDTH = MIX_WIDTH - RWKV_WIDTH
HGRN_EXPAND = 128
HGRN_HEADS = HGRN_WIDTH // HGRN_EXPAND
HGRN_PROJ = 4 * HGRN_WIDTH
AB_PROJ = RWKV_PROJ + HGRN_PROJ
SSM_INNER = 2 * D_MODEL
SSM_HEAD_DIM = 64
SSM_HEADS = SSM_INNER // SSM_HEAD_DIM
SSM_STATE = 128
SSM_GROUPS = 4
SSM_HPG = SSM_HEADS // SSM_GROUPS
CONV_WIDTH = 4
CONV_DIM = SSM_INNER + 2 * SSM_GROUPS * SSM_STATE
SSM_PROJ = SSM_INNER + CONV_DIM + SSM_HEADS
CHUNK = 64
PEER_HEADS = 8
PEER_KEYS = 128
PEER_EXPERTS = PEER_KEYS * PEER_KEYS
PEER_KEY_DIM = 256
PEER_HALF = PEER_KEY_DIM // 2
PEER_TOPK = 16
RMS_EPS = 1e-6
GN_EPS = 64e-5
L2_EPS = 1e-12
F32 = jnp.float32
BF16 = jnp.bfloat16

LANES = 128
SUBLANES = 8
VMEM_LIMIT_BYTES = 56 * 1024 * 1024

PEER_I1_PER_CHUNK = 8
PEER_EXPERT_CHUNK = PEER_I1_PER_CHUNK * PEER_KEYS
PEER_ROW_TILE = 32
PEER_RANKS = PEER_TOPK + 1
PEER_RANK_ROWS = 24
MASKED = -1e30


def _pick_tile(n, candidates):
    for c in candidates:
        if n % c == 0:
            return c
    return n


def _norm_matmul_kernel(x_ref, nw_ref, w_ref, o_ref):
    x = x_ref[...]
    xn = x * lax.rsqrt(jnp.mean(x * x, axis=-1, keepdims=True) + RMS_EPS) * nw_ref[...]
    o_ref[...] = jnp.dot(xn.astype(BF16), w_ref[...], preferred_element_type=F32)


def norm_matmul(x, norm_w, w_bf16):
    m, k = x.shape
    n = w_bf16.shape[1]
    tm = _pick_tile(m, (512, 256, 128))
    tn = _pick_tile(n, (1024, 896, 768, 640, 512, 384, 256, 128))
    return pl.pallas_call(
        _norm_matmul_kernel,
        grid=(m // tm, n // tn),
        in_specs=[pl.BlockSpec((tm, k), lambda i, j: (i, 0)),
                  pl.BlockSpec((1, k), lambda i, j: (0, 0)),
                  pl.BlockSpec((k, tn), lambda i, j: (0, j))],
        out_specs=pl.BlockSpec((tm, tn), lambda i, j: (i, j)),
        out_shape=jax.ShapeDtypeStruct((m, n), F32),
        compiler_params=pltpu.CompilerParams(
            dimension_semantics=("arbitrary", "arbitrary"), vmem_limit_bytes=VMEM_LIMIT_BYTES),
        name="norm_matmul",
    )(x, norm_w.reshape(1, k), w_bf16)


def _matmul_residual_kernel(a_ref, w_ref, r_ref, o_ref):
    o_ref[...] = r_ref[...] + jnp.dot(a_ref[...].astype(BF16), w_ref[...], preferred_element_type=F32)


def matmul_residual(a, w_bf16, res):
    m, k = a.shape
    n = w_bf16.shape[1]
    tm = _pick_tile(m, (512, 256, 128))
    tn = _pick_tile(n, (1024, 512, 256, 128))
    return pl.pallas_call(
        _matmul_residual_kernel,
        grid=(m // tm, n // tn),
        in_specs=[pl.BlockSpec((tm, k), lambda i, j: (i, 0)),
                  pl.BlockSpec((k, tn), lambda i, j: (0, j)),
                  pl.BlockSpec((tm, tn), lambda i, j: (i, j))],
        out_specs=pl.BlockSpec((tm, tn), lambda i, j: (i, j)),
        out_shape=jax.ShapeDtypeStruct((m, n), F32),
        compiler_params=pltpu.CompilerParams(
            dimension_semantics=("arbitrary", "arbitrary"), vmem_limit_bytes=VMEM_LIMIT_BYTES),
        name="matmul_residual",
    )(a, w_bf16, res)


def _rms_norm_kernel(x_ref, nw_ref, o_ref):
    x = x_ref[...]
    o_ref[...] = x * lax.rsqrt(jnp.mean(x * x, axis=-1, keepdims=True) + RMS_EPS) * nw_ref[...]


def rms_norm_rows(x, norm_w):
    m, k = x.shape
    tm = _pick_tile(m, (512, 256, 128))
    return pl.pallas_call(
        _rms_norm_kernel,
        grid=(m // tm,),
        in_specs=[pl.BlockSpec((tm, k), lambda i: (i, 0)), pl.BlockSpec((1, k), lambda i: (0, 0))],
        out_specs=pl.BlockSpec((tm, k), lambda i: (i, 0)),
        out_shape=jax.ShapeDtypeStruct((m, k), F32),
        compiler_params=pltpu.CompilerParams(dimension_semantics=("arbitrary",)),
        name="rms_norm",
    )(x, norm_w.reshape(1, k))


def _top_values(s, sv_ref):
    work = s
    for k in range(PEER_RANKS):
        m = jnp.max(work, axis=0, keepdims=True)
        sv_ref[k:k + 1, :] = m
        work = jnp.where(work == m, MASKED, work)


def _peer_route(h, q_ref, keys_ref, sv1_ref, sv2_ref, cand_ref, thr1_ref, c1_ref, t2_ref, e2_ref, tb):
    q1 = q_ref[pl.ds(pl.multiple_of(h * PEER_KEY_DIM, PEER_KEY_DIM), PEER_HALF), :]
    q2 = q_ref[pl.ds(pl.multiple_of(h * PEER_KEY_DIM + PEER_HALF, PEER_HALF), PEER_HALF), :]
    s1 = jnp.dot(keys_ref[h], q1, preferred_element_type=F32)
    s2 = jnp.dot(keys_ref[PEER_HEADS + h], q2, preferred_element_type=F32)
    pad = jnp.full((PEER_RANK_ROWS - PEER_RANKS, tb), MASKED, F32)
    sv1_ref[PEER_RANKS:, :] = pad
    sv2_ref[PEER_RANKS:, :] = pad
    _top_values(s1, sv1_ref)
    _top_values(s2, sv2_ref)
    max1 = sv1_ref[0:1, :]
    max2 = sv2_ref[0:1, :]
    t1 = s1 - max1
    t2 = s2 - max2
    tv2 = jnp.where(sv2_ref[...] > 0.5 * MASKED, sv2_ref[...] - max2, MASKED)
    for a in range(PEER_RANKS):
        cand_ref[a * PEER_RANK_ROWS:(a + 1) * PEER_RANK_ROWS, :] = (sv1_ref[a:a + 1, :] - max1) + tv2
    work = cand_ref[...]
    z = jnp.zeros((1, tb), F32)
    m_prev = None
    m = None
    for k in range(PEER_RANKS):
        m_prev = m
        m = jnp.max(work, axis=0, keepdims=True)
        if k < PEER_TOPK:
            z = z + jnp.exp(m)
            work = jnp.where(work == m, MASKED, work)
    thr = 0.5 * (m_prev + m)
    thr1_ref[h] = (thr - t1).reshape(thr1_ref.shape[1:])
    c1_ref[h] = (jnp.exp(t1) / z).reshape(c1_ref.shape[1:])
    t2_ref[h] = t2
    e2_ref[h] = jnp.exp(t2)


def _peer_kernel(x_ref, nw_ref, wqt_ref, keys_ref, u_ref, vt_ref, y_ref,
                 xnt_ref, q_ref, sv1_ref, sv2_ref, cand_ref, thr1_ref, c1_ref, t2_ref, e2_ref,
                 acc_ref, a_ref, h_ref, *, tb):
    j = pl.program_id(1)

    @pl.when(j == 0)
    def _():
        x = x_ref[...]
        xn = x * lax.rsqrt(jnp.mean(x * x, axis=-1, keepdims=True) + RMS_EPS) * nw_ref[...]
        xnt_ref[...] = xn.T.astype(BF16)
        q_ref[...] = jnp.dot(wqt_ref[...], xnt_ref[...], preferred_element_type=F32).astype(BF16)
        acc_ref[...] = jnp.zeros_like(acc_ref)

        def head(h, carry):
            _peer_route(h, q_ref, keys_ref, sv1_ref, sv2_ref, cand_ref, thr1_ref, c1_ref, t2_ref, e2_ref, tb)
            return carry

        lax.fori_loop(0, PEER_HEADS, head, 0)

    a_ref[...] = jnp.dot(u_ref[...], xnt_ref[...], preferred_element_type=F32)

    for i1l in range(PEER_I1_PER_CHUNK):
        def rows(r, carry, i1l=i1l):
            r0 = pl.multiple_of(r * PEER_ROW_TILE, PEER_ROW_TILE)
            w = jnp.zeros((PEER_ROW_TILE, tb), F32)
            for h in range(PEER_HEADS):
                thr = thr1_ref[h, j, i1l:i1l + 1, :]
                c1 = c1_ref[h, j, i1l:i1l + 1, :]
                t2 = t2_ref[h, pl.ds(r0, PEER_ROW_TILE), :]
                e2 = e2_ref[h, pl.ds(r0, PEER_ROW_TILE), :]
                w = w + jnp.where(t2 >= thr, c1 * e2, 0.0)
            a = a_ref[pl.ds(i1l * PEER_KEYS + r0, PEER_ROW_TILE), :]
            g = 0.5 * a * (1.0 + lax.erf(a * (1.0 / math.sqrt(2.0))))
            h_ref[pl.ds(i1l * PEER_KEYS + r0, PEER_ROW_TILE), :] = (w * g).astype(BF16)
            return carry

        lax.fori_loop(0, PEER_KEYS // PEER_ROW_TILE, rows, 0)

    acc_ref[...] += jnp.dot(vt_ref[...], h_ref[...], preferred_element_type=F32)

    @pl.when(j == pl.num_programs(1) - 1)
    def _():
        y_ref[...] = x_ref[...] + acc_ref[...].T


def peer_block(x, norm_w, wqt_bf16, keys_bf16, u_bf16, vt_bf16):
    t, d = x.shape
    tb = _pick_tile(t, (512, 256, 128))
    n_chunks = PEER_EXPERTS // PEER_EXPERT_CHUNK
    n_groups = PEER_KEYS // PEER_I1_PER_CHUNK
    const = dict(pipeline_mode=pl.Buffered(1))
    scratch = [
        pltpu.VMEM((d, tb), BF16),
        pltpu.VMEM((PEER_HEADS * PEER_KEY_DIM, tb), BF16),
        pltpu.VMEM((PEER_RANK_ROWS, tb), F32),
        pltpu.VMEM((PEER_RANK_ROWS, tb), F32),
        pltpu.VMEM((PEER_RANKS * PEER_RANK_ROWS, tb), F32),
        pltpu.VMEM((PEER_HEADS, n_groups, PEER_I1_PER_CHUNK, tb), F32),
        pltpu.VMEM((PEER_HEADS, n_groups, PEER_I1_PER_CHUNK, tb), F32),
        pltpu.VMEM((PEER_HEADS, PEER_KEYS, tb), F32),
        pltpu.VMEM((PEER_HEADS, PEER_KEYS, tb), F32),
        pltpu.VMEM((d, tb), F32),
        pltpu.VMEM((PEER_EXPERT_CHUNK, tb), F32),
        pltpu.VMEM((PEER_EXPERT_CHUNK, tb), BF16),
    ]
    return pl.pallas_call(
        functools.partial(_peer_kernel, tb=tb),
        grid=(t // tb, n_chunks),
        in_specs=[pl.BlockSpec((tb, d), lambda i, j: (i, 0)),
                  pl.BlockSpec((1, d), lambda i, j: (0, 0)),
                  pl.BlockSpec((PEER_HEADS * PEER_KEY_DIM, d), lambda i, j: (0, 0), **const),
                  pl.BlockSpec((2 * PEER_HEADS, PEER_KEYS, PEER_HALF), lambda i, j: (0, 0, 0), **const),
                  pl.BlockSpec((PEER_EXPERT_CHUNK, d), lambda i, j: (j, 0)),
                  pl.BlockSpec((d, PEER_EXPERT_CHUNK), lambda i, j: (0, j))],
        out_specs=pl.BlockSpec((tb, d), lambda i, j: (i, 0)),
        out_shape=jax.ShapeDtypeStruct((t, d), F32),
        scratch_shapes=scratch,
        compiler_params=pltpu.CompilerParams(
            dimension_semantics=("arbitrary", "arbitrary"), vmem_limit_bytes=VMEM_LIMIT_BYTES),
        name="peer_block",
    )(x, norm_w.reshape(1, d), wqt_bf16, keys_bf16, u_bf16, vt_bf16)


RWKV_VALUE_HALF = RWKV_HEAD_DIM // 2
RWKV_KEY_ROWS = 5
RWKV_TIME_TILE = 32
_ROW_W, _ROW_KK, _ROW_KKA, _ROW_K, _ROW_R = range(RWKV_KEY_ROWS)


def _tree_sum(parts):
    while len(parts) > 1:
        parts = [parts[i] + parts[i + 1] for i in range(0, len(parts) - 1, 2)] + (
            [parts[-1]] if len(parts) % 2 else [])
    return parts[0]


def _rwkv_scan_kernel(kv_ref, v_ref, s0_ref, o_ref, s_ref, *, tc):
    @pl.when(pl.program_id(1) == 0)
    def _():
        s_ref[...] = s0_ref[...]

    n_acc = 4

    def step(tt, carry):
        acc = [None] * n_acc
        for k in range(RWKV_HEAD_DIM):
            term = s_ref[k] * kv_ref[tt, _ROW_KK, k:k + 1, :]
            acc[k % n_acc] = term if acc[k % n_acc] is None else acc[k % n_acc] + term
        sa = -_tree_sum(acc)
        v = v_ref[tt]
        oacc = [None] * n_acc
        for k in range(RWKV_HEAD_DIM):
            s_new = (s_ref[k] * kv_ref[tt, _ROW_W, k:k + 1, :] + sa * kv_ref[tt, _ROW_KKA, k:k + 1, :]
                     + v * kv_ref[tt, _ROW_K, k:k + 1, :])
            s_ref[k] = s_new
            term = s_new * kv_ref[tt, _ROW_R, k:k + 1, :]
            oacc[k % n_acc] = term if oacc[k % n_acc] is None else oacc[k % n_acc] + term
        o_ref[tt] = _tree_sum(oacc)
        return carry

    lax.fori_loop(0, tc, step, 0)


def rwkv_scan(r, decay, k, v, kk, a, s0):
    b, l, h, n = r.shape
    lanes = b * h * 2
    vh = RWKV_VALUE_HALF

    def key_rows(x):
        x = jnp.transpose(x, (1, 3, 0, 2))
        return jnp.broadcast_to(x[..., None], (l, n, b, h, 2)).reshape(l, n, lanes)

    kv = jnp.stack([key_rows(decay), key_rows(kk), key_rows(kk * a), key_rows(k), key_rows(r)], axis=1)
    vv = jnp.transpose(v.reshape(b, l, h, 2, vh), (1, 4, 0, 2, 3)).reshape(l, vh, lanes)
    st = jnp.transpose(s0.reshape(b, h, 2, vh, n), (4, 3, 0, 1, 2)).reshape(n, vh, lanes)
    tc = _pick_tile(l, (RWKV_TIME_TILE,))
    o, s_fin = pl.pallas_call(
        functools.partial(_rwkv_scan_kernel, tc=tc),
        grid=(lanes // LANES, l // tc),
        in_specs=[pl.BlockSpec((tc, RWKV_KEY_ROWS, n, LANES), lambda g, t: (t, 0, 0, g)),
                  pl.BlockSpec((tc, vh, LANES), lambda g, t: (t, 0, g)),
                  pl.BlockSpec((n, vh, LANES), lambda g, t: (0, 0, g))],
        out_specs=[pl.BlockSpec((tc, vh, LANES), lambda g, t: (t, 0, g)),
                   pl.BlockSpec((n, vh, LANES), lambda g, t: (0, 0, g))],
        out_shape=[jax.ShapeDtypeStruct((l, vh, lanes), F32), jax.ShapeDtypeStruct((n, vh, lanes), F32)],
        compiler_params=pltpu.CompilerParams(
            dimension_semantics=("arbitrary", "arbitrary"), vmem_limit_bytes=VMEM_LIMIT_BYTES),
        name="rwkv_scan",
    )(kv, vv, st)
    o = jnp.transpose(o.reshape(l, vh, b, h, 2), (2, 0, 3, 4, 1)).reshape(b, l, h, n)
    s_fin = jnp.transpose(s_fin.reshape(n, vh, b, h, 2), (2, 3, 4, 1, 0)).reshape(b, h, n, n)
    return o, s_fin


def _to_chunks(a, c):
    b, l = a.shape[:2]
    return jnp.swapaxes(a.reshape((b, l // c, c) + a.shape[2:]), 0, 1)


def _from_chunks(a):
    n, b, c = a.shape[:3]
    return jnp.swapaxes(a, 0, 1).reshape((b, n * c) + a.shape[3:])


def _masked_decay(diff, mask):
    return jnp.where(mask, jnp.exp(jnp.where(mask, diff, 0.0)), 0.0)


def _rwkv7_mixer(p, shift_prev, s0, mu, w0, w_up, a0, a_up, g_up, k_k, k_a, r_k, gn_w, gn_b):
    b, l, _ = p.shape
    prev = jnp.concatenate([shift_prev[:, None], p[:, :-1]], axis=1)
    m = p + mu * (prev - p)
    r, k, v, wd, ad, gd = jnp.split(m, RWKV_SPLITS, axis=-1)
    w_raw = w0 + jnp.tanh(wd) @ w_up
    decay = jnp.exp(-jnp.exp(-jax.nn.softplus(-w_raw) - 0.5))
    a = jax.nn.sigmoid(a0 + ad @ a_up)
    g = jax.nn.sigmoid(gd) @ g_up
    heads = lambda t: t.reshape(b, l, RWKV_HEADS, RWKV_HEAD_DIM)
    kk = heads(k * k_k)
    kk = kk / jnp.maximum(jnp.sqrt(jnp.sum(kk * kk, axis=-1, keepdims=True)), L2_EPS)
    k = heads(k * (1.0 + (a - 1.0) * k_a))
    r, v, a, decay = heads(r), heads(v), heads(a), heads(decay)

    o, s_fin = rwkv_scan(r, decay, k, v, kk, a, s0)
    mean = jnp.mean(o, axis=-1, keepdims=True)
    var = jnp.mean(jnp.square(o - mean), axis=-1, keepdims=True)
    o = ((o - mean) * lax.rsqrt(var + GN_EPS)).reshape(b, l, RWKV_WIDTH) * gn_w + gn_b
    bonus = jnp.sum(r * k * r_k, axis=-1, keepdims=True) * v
    out = (o + bonus.reshape(b, l, RWKV_WIDTH)) * g
    return out, p[:, -1], s_fin


def _chunked_gla(q, k, v, g, s0):
    c = math.gcd(q.shape[1], CHUNK)
    mask = jnp.tril(jnp.ones((c, c), dtype=bool))[None, :, :, None, None]

    def step(s, inp):
        qc, kc, vc, gc = inp
        gcum = jnp.cumsum(gc, axis=1)
        rel = _masked_decay(gcum[:, :, None] - gcum[:, None, :], mask)
        att = jnp.einsum('bthk,btshk->btsh', qc, rel * kc[:, None])
        o = (jnp.einsum('btsh,bshv->bthv', att, vc)
             + jnp.einsum('bthk,bhkv->bthv', qc * jnp.exp(gcum), s))
        g_end = gcum[:, -1]
        s = s * jnp.exp(g_end)[..., None] + jnp.einsum(
            'bshk,bshv->bhkv', kc * jnp.exp(g_end[:, None] - gcum), vc)
        return s, o

    s_fin, o = lax.scan(step, s0, tuple(_to_chunks(t, c) for t in (q, k, v, g)))
    return _from_chunks(o), s_fin


def _hgrn2_mixer(p, s0, lb, norm_w):
    b, l, _ = p.shape
    q, f, i, og = jnp.split(p, 4, axis=-1)
    fg = lb + (1.0 - lb) * jax.nn.sigmoid(f)
    log_f = jnp.log(fg)
    k = 1.0 - fg
    heads = lambda t: t.reshape(b, l, HGRN_HEADS, HGRN_EXPAND)
    o, s_fin = _chunked_gla(heads(jax.nn.silu(q)), heads(k), heads(i), heads(log_f), s0)
    o = o * lax.rsqrt(jnp.mean(o * o, axis=-1, keepdims=True) + RMS_EPS) * norm_w
    out = o.reshape(b, l, HGRN_WIDTH) * jax.nn.silu(og)
    return out, s_fin


def _chunked_ssd(x, bm, cm, log_a, s0):
    c = math.gcd(x.shape[1], CHUNK)
    mask = jnp.tril(jnp.ones((c, c), dtype=bool))[None, :, :, None, None]

    def step(s, inp):
        xc, bc, cc, ac = inp
        acum = jnp.cumsum(ac, axis=1)
        lmat = _masked_decay(acum[:, :, None] - acum[:, None, :], mask)
        cb = jnp.einsum('btgn,bsgn->btsg', cc, bc)
        y = jnp.einsum('btsgr,bsgrp->btgrp', cb[..., None] * lmat, xc)
        y = y + jnp.einsum('btgn,bgrpn->btgrp', cc, s) * jnp.exp(acum)[..., None]
        a_end = acum[:, -1]
        s = s * jnp.exp(a_end)[..., None, None] + jnp.einsum(
            'bsgn,bsgrp->bgrpn', bc, xc * jnp.exp(a_end[:, None] - acum)[..., None])
        return s, y

    s_fin, y = lax.scan(step, s0, tuple(_to_chunks(t, c) for t in (x, bm, cm, log_a)))
    return _from_chunks(y), s_fin


def _mamba2_mixer(p, conv_buf, s0, conv_w, conv_b, dt_bias, a_log, d_skip, norm_w):
    b, l, _ = p.shape
    z, xbc, dt = jnp.split(p, [SSM_INNER, SSM_INNER + CONV_DIM], axis=-1)
    xpad = jnp.concatenate([conv_buf, xbc], axis=1)
    conv = conv_b + xpad[:, 0:l] * conv_w[0]
    for j in range(1, CONV_WIDTH):
        conv = conv + xpad[:, j:j + l] * conv_w[j]
    xbc = jax.nn.silu(conv)
    new_buf = xpad[:, -(CONV_WIDTH - 1):]
    xs, bm, cm = jnp.split(xbc, [SSM_INNER, SSM_INNER + SSM_GROUPS * SSM_STATE], axis=-1)
    xs = xs.reshape(b, l, SSM_GROUPS, SSM_HPG, SSM_HEAD_DIM)
    bm = bm.reshape(b, l, SSM_GROUPS, SSM_STATE)
    cm = cm.reshape(b, l, SSM_GROUPS, SSM_STATE)
    dt = jax.nn.softplus(dt + dt_bias).reshape(b, l, SSM_GROUPS, SSM_HPG)
    log_a = dt * (-jnp.exp(a_log)).reshape(SSM_GROUPS, SSM_HPG)
    s0 = s0.reshape(b, SSM_GROUPS, SSM_HPG, SSM_HEAD_DIM, SSM_STATE)
    y, s_fin = _chunked_ssd(xs * dt[..., None], bm, cm, log_a, s0)
    y = y + xs * d_skip.reshape(SSM_GROUPS, SSM_HPG, 1)
    y = y.reshape(b, l, SSM_INNER) * jax.nn.silu(z)
    yg = y.reshape(b, l, SSM_GROUPS, SSM_INNER // SSM_GROUPS)
    yg = yg * lax.rsqrt(jnp.mean(yg * yg, axis=-1, keepdims=True) + RMS_EPS)
    out = yg.reshape(b, l, SSM_INNER) * norm_w
    return out, new_buf, s_fin.reshape(b, SSM_HEADS, SSM_HEAD_DIM, SSM_STATE)


def _prepare_weights(w):
    ssm_pad = (-SSM_PROJ) % (6 * LANES)
    return dict(
        ab_w_in=w['ab_w_in'].astype(BF16),
        ab_w_out=w['ab_w_out'].astype(BF16),
        ssm_w_in=jnp.pad(w['ssm_w_in'], ((0, 0), (0, 0), (0, ssm_pad))).astype(BF16),
        ssm_w_out=w['ssm_w_out'].astype(BF16),
        peer_wqt=jnp.swapaxes(w['peer_w_q'], 1, 2).astype(BF16),
        peer_keys=w['peer_sub_keys'].reshape(DEPTH, 2 * PEER_HEADS, PEER_KEYS, PEER_HALF).astype(BF16),
        peer_u=w['peer_u'].astype(BF16),
        peer_vt=jnp.swapaxes(w['peer_v'], 1, 2).astype(BF16),
    )


def _run_trunk(x, st_shift, st_rwkv, st_hgrn, st_ssm, st_conv, w, wp):
    b, l, d = x.shape
    t = b * l
    lb = jax.nn.softmax(w['hgrn_lower_bounds'], axis=0)
    lb = jnp.cumsum(lb, axis=0) - lb[0]
    n_shift, n_rwkv, n_hgrn, n_ssm, n_conv = [], [], [], [], []
    xt = x.reshape(t, d)
    for i in range(DEPTH):
        j = i // 2
        if i % 2 == 0:
            proj = norm_matmul(xt, w['norm_mix'][i], wp['ab_w_in'][j]).reshape(b, l, AB_PROJ)
            o_a, sh, sa = _rwkv7_mixer(proj[..., :RWKV_PROJ], st_shift[j], st_rwkv[j],
                                       w['rwkv_mu'][j], w['rwkv_w0'][j], w['rwkv_w_up'][j], w['rwkv_a0'][j],
                                       w['rwkv_a_up'][j], w['rwkv_g_up'][j], w['rwkv_k_k'][j], w['rwkv_k_a'][j],
                                       w['rwkv_r_k'][j], w['rwkv_gn_w'][j], w['rwkv_gn_b'][j])
            o_b, sb = _hgrn2_mixer(proj[..., RWKV_PROJ:], st_hgrn[j], lb[j], w['hgrn_norm_w'][j])
            mix_in = jnp.concatenate([o_a, o_b], axis=-1).reshape(t, MIX_WIDTH)
            xt = matmul_residual(mix_in, wp['ab_w_out'][j], xt)
            n_shift.append(sh); n_rwkv.append(sa); n_hgrn.append(sb)
        else:
            proj = norm_matmul(xt, w['norm_mix'][i], wp['ssm_w_in'][j])[:, :SSM_PROJ].reshape(b, l, SSM_PROJ)
            o_c, cb, sc = _mamba2_mixer(proj, st_conv[j], st_ssm[j], w['ssm_conv_w'][j], w['ssm_conv_b'][j],
                                        w['ssm_dt_bias'][j], w['ssm_a_log'][j], w['ssm_d'][j], w['ssm_norm_w'][j])
            xt = matmul_residual(o_c.reshape(t, SSM_INNER), wp['ssm_w_out'][j], xt)
            n_conv.append(cb); n_ssm.append(sc)
        xt = peer_block(xt, w['norm_ffn'][i], wp['peer_wqt'][i], wp['peer_keys'][i],
                        wp['peer_u'][i], wp['peer_vt'][i])
    y = rms_norm_rows(xt, w['norm_final']).reshape(b, l, d)
    return (y, jnp.stack(n_shift), jnp.stack(n_rwkv), jnp.stack(n_hgrn), jnp.stack(n_ssm), jnp.stack(n_conv))


def kernel(x_prompt, x_sample, state_rwkv_shift, state_rwkv, state_hgrn, state_ssm, state_conv,
           norm_mix, norm_ffn, norm_final, ab_w_in, ab_w_out, rwkv_mu, rwkv_w0, rwkv_w_up, rwkv_a0,
           rwkv_a_up, rwkv_g_up, rwkv_k_k, rwkv_k_a, rwkv_r_k, rwkv_gn_w, rwkv_gn_b, hgrn_lower_bounds,
           hgrn_norm_w, ssm_w_in, ssm_conv_w, ssm_conv_b, ssm_dt_bias, ssm_a_log, ssm_d, ssm_norm_w,
           ssm_w_out, peer_w_q, peer_sub_keys, peer_u, peer_v):
    w = dict(norm_mix=norm_mix, norm_ffn=norm_ffn, norm_final=norm_final, ab_w_in=ab_w_in, ab_w_out=ab_w_out,
             rwkv_mu=rwkv_mu, rwkv_w0=rwkv_w0, rwkv_w_up=rwkv_w_up, rwkv_a0=rwkv_a0, rwkv_a_up=rwkv_a_up,
             rwkv_g_up=rwkv_g_up, rwkv_k_k=rwkv_k_k, rwkv_k_a=rwkv_k_a, rwkv_r_k=rwkv_r_k, rwkv_gn_w=rwkv_gn_w,
             rwkv_gn_b=rwkv_gn_b, hgrn_lower_bounds=hgrn_lower_bounds, hgrn_norm_w=hgrn_norm_w,
             ssm_w_in=ssm_w_in, ssm_conv_w=ssm_conv_w, ssm_conv_b=ssm_conv_b, ssm_dt_bias=ssm_dt_bias,
             ssm_a_log=ssm_a_log, ssm_d=ssm_d, ssm_norm_w=ssm_norm_w, ssm_w_out=ssm_w_out,
             peer_w_q=peer_w_q, peer_sub_keys=peer_sub_keys, peer_u=peer_u, peer_v=peer_v)
    wp = _prepare_weights(w)
    bp = x_prompt.shape[0]
    prompt = _run_trunk(
        x_prompt,
        jnp.zeros((DEPTH // 2, bp, RWKV_PROJ), F32),
        jnp.zeros((DEPTH // 2, bp, RWKV_HEADS, RWKV_HEAD_DIM, RWKV_HEAD_DIM), F32),
        jnp.zeros((DEPTH // 2, bp, HGRN_HEADS, HGRN_EXPAND, HGRN_EXPAND), F32),
        jnp.zeros((DEPTH // 2, bp, SSM_HEADS, SSM_HEAD_DIM, SSM_STATE), F32),
        jnp.zeros((DEPTH // 2, bp, CONV_WIDTH - 1, CONV_DIM), F32),
        w, wp)
    sample = _run_trunk(x_sample, state_rwkv_shift, state_rwkv, state_hgrn, state_ssm, state_conv, w, wp)
    return (prompt[0], sample[0]) + prompt[1:] + sample[1:]
```

```python
import functools
import math

import jax
import jax.numpy as jnp
from jax import lax
from jax.experimental import pallas as pl
from jax.experimental.pallas import tpu as pltpu

D_MODEL = 1024
DEPTH = 4
MIX_WIDTH = D_MODEL
RWKV_WIDTH = MIX_WIDTH // 2
RWKV_HEAD_DIM = 64
RWKV_HEADS = RWKV_WIDTH // RWKV_HEAD_DIM
DECAY_LORA = 64
AAA_LORA = 64
GATE_LORA = 128
RWKV_SPLITS = (RWKV_WIDTH, 2 * RWKV_WIDTH, 3 * RWKV_WIDTH, 3 * RWKV_WIDTH + DECAY_LORA,
               3 * RWKV_WIDTH + DECAY_LORA + AAA_LORA)
RWKV_PROJ = 3 * RWKV_WIDTH + DECAY_LORA + AAA_LORA + GATE_LORA
HGRN_WIDTH = MIX_WIDTH - RWKV_WIDTH
HGRN_EXPAND = 128
HGRN_HEADS = HGRN_WIDTH // HGRN_EXPAND
HGRN_PROJ = 4 * HGRN_WIDTH
AB_PROJ = RWKV_PROJ + HGRN_PROJ
SSM_INNER = 2 * D_MODEL
SSM_HEAD_DIM = 64
SSM_HEADS = SSM_INNER // SSM_HEAD_DIM
SSM_STATE = 128
SSM_GROUPS = 4
SSM_HPG = SSM_HEADS // SSM_GROUPS
CONV_WIDTH = 4
CONV_DIM = SSM_INNER + 2 * SSM_GROUPS * SSM_STATE
SSM_PROJ = SSM_INNER + CONV_DIM + SSM_HEADS
CHUNK = 64
PEER_HEADS = 8
PEER_KEYS = 128
PEER_EXPERTS = PEER_KEYS * PEER_KEYS
PEER_KEY_DIM = 256
PEER_HALF = PEER_KEY_DIM // 2
PEER_TOPK = 16
RMS_EPS = 1e-6
GN_EPS = 64e-5
L2_EPS = 1e-12
F32 = jnp.float32
BF16 = jnp.bfloat16

LANES = 128
SUBLANES = 8
VMEM_LIMIT_BYTES = 56 * 1024 * 1024

PEER_I1_PER_CHUNK = 8
PEER_EXPERT_CHUNK = PEER_I1_PER_CHUNK * PEER_KEYS
PEER_ROW_TILE = 32
PEER_SUB_CHUNK = 2 * PEER_KEYS
PEER_CHUNKS_PER_STEP = 2
PEER_RANKS = PEER_TOPK + 1
PEER_RANK_ROWS = 24
MASKED = -1e30


def _pick_tile(n, candidates):
    for c in candidates:
        if n % c == 0:
            return c
    return n


def _norm_matmul_kernel(x_ref, nw_ref, w_ref, o_ref):
    x = x_ref[...]
    xn = x * lax.rsqrt(jnp.mean(x * x, axis=-1, keepdims=True) + RMS_EPS) * nw_ref[...]
    o_ref[...] = jnp.dot(xn.astype(BF16), w_ref[...], preferred_element_type=F32)


def norm_matmul(x, norm_w, w_bf16):
    m, k = x.shape
    n = w_bf16.shape[1]
    tm = _pick_tile(m, (512, 256, 128))
    tn = _pick_tile(n, (1024, 896, 768, 640, 512, 384, 256, 128))
    return pl.pallas_call(
        _norm_matmul_kernel,
        grid=(m // tm, n // tn),
        in_specs=[pl.BlockSpec((tm, k), lambda i, j: (i, 0)),
                  pl.BlockSpec((1, k), lambda i, j: (0, 0)),
                  pl.BlockSpec((k, tn), lambda i, j: (0, j))],
        out_specs=pl.BlockSpec((tm, tn), lambda i, j: (i, j)),
        out_shape=jax.ShapeDtypeStruct((m, n), F32),
        compiler_params=pltpu.CompilerParams(
            dimension_semantics=("arbitrary", "arbitrary"), vmem_limit_bytes=VMEM_LIMIT_BYTES),
        name="norm_matmul",
    )(x, norm_w.reshape(1, k), w_bf16)


def _matmul_residual_kernel(a_ref, w_ref, r_ref, o_ref):
    o_ref[...] = r_ref[...] + jnp.dot(a_ref[...].astype(BF16), w_ref[...], preferred_element_type=F32)


def matmul_residual(a, w_bf16, res):
    m, k = a.shape
    n = w_bf16.shape[1]
    tm = _pick_tile(m, (512, 256, 128))
    tn = _pick_tile(n, (1024, 512, 256, 128))
    return pl.pallas_call(
        _matmul_residual_kernel,
        grid=(m // tm, n // tn),
        in_specs=[pl.BlockSpec((tm, k), lambda i, j: (i, 0)),
                  pl.BlockSpec((k, tn), lambda i, j: (0, j)),
                  pl.BlockSpec((tm, tn), lambda i, j: (i, j))],
        out_specs=pl.BlockSpec((tm, tn), lambda i, j: (i, j)),
        out_shape=jax.ShapeDtypeStruct((m, n), F32),
        compiler_params=pltpu.CompilerParams(
            dimension_semantics=("arbitrary", "arbitrary"), vmem_limit_bytes=VMEM_LIMIT_BYTES),
        name="matmul_residual",
    )(a, w_bf16, res)


def _rms_norm_kernel(x_ref, nw_ref, o_ref):
    x = x_ref[...]
    o_ref[...] = x * lax.rsqrt(jnp.mean(x * x, axis=-1, keepdims=True) + RMS_EPS) * nw_ref[...]


def rms_norm_rows(x, norm_w):
    m, k = x.shape
    tm = _pick_tile(m, (512, 256, 128))
    return pl.pallas_call(
        _rms_norm_kernel,
        grid=(m // tm,),
        in_specs=[pl.BlockSpec((tm, k), lambda i: (i, 0)), pl.BlockSpec((1, k), lambda i: (0, 0))],
        out_specs=pl.BlockSpec((tm, k), lambda i: (i, 0)),
        out_shape=jax.ShapeDtypeStruct((m, k), F32),
        compiler_params=pltpu.CompilerParams(dimension_semantics=("arbitrary",)),
        name="rms_norm",
    )(x, norm_w.reshape(1, k))


def _top_values(s, sv_ref):
    work = s
    for k in range(PEER_RANKS):
        m = jnp.max(work, axis=0, keepdims=True)
        sv_ref[k:k + 1, :] = m
        work = jnp.where(work == m, MASKED, work)


def _peer_route(h, q_ref, keys_ref, sv1_ref, sv2_ref, cand_ref, thr1_ref, c1_ref, t2_ref, e2_ref, tb):
    q1 = q_ref[pl.ds(pl.multiple_of(h * PEER_KEY_DIM, PEER_KEY_DIM), PEER_HALF), :]
    q2 = q_ref[pl.ds(pl.multiple_of(h * PEER_KEY_DIM + PEER_HALF, PEER_HALF), PEER_HALF), :]
    s1 = jnp.dot(keys_ref[h], q1, preferred_element_type=F32)
    s2 = jnp.dot(keys_ref[PEER_HEADS + h], q2, preferred_element_type=F32)
    pad = jnp.full((PEER_RANK_ROWS - PEER_RANKS, tb), MASKED, F32)
    sv1_ref[PEER_RANKS:, :] = pad
    sv2_ref[PEER_RANKS:, :] = pad
    _top_values(s1, sv1_ref)
    _top_values(s2, sv2_ref)
    max1 = sv1_ref[0:1, :]
    max2 = sv2_ref[0:1, :]
    t1 = s1 - max1
    t2 = s2 - max2
    tv2 = jnp.where(sv2_ref[...] > 0.5 * MASKED, sv2_ref[...] - max2, MASKED)
    for a in range(PEER_RANKS):
        cand_ref[a * PEER_RANK_ROWS:(a + 1) * PEER_RANK_ROWS, :] = (sv1_ref[a:a + 1, :] - max1) + tv2
    work = cand_ref[...]
    z = jnp.zeros((1, tb), F32)
    m_prev = None
    m = None
    for k in range(PEER_RANKS):
        m_prev = m
        m = jnp.max(work, axis=0, keepdims=True)
        if k < PEER_TOPK:
            z = z + jnp.exp(m)
            work = jnp.where(work == m, MASKED, work)
    thr = 0.5 * (m_prev + m)
    thr1_ref[h] = (thr - t1).reshape(thr1_ref.shape[1:])
    c1_ref[h] = (jnp.exp(t1) / z).reshape(c1_ref.shape[1:])
    t2_ref[h] = t2
    e2_ref[h] = jnp.exp(t2)


def _peer_kernel(x_ref, nw_ref, wqt_ref, keys_ref, u_ref, vt_ref, y_ref,
                 xnt_ref, q_ref, sv1_ref, sv2_ref, cand_ref, thr1_ref, c1_ref, t2_ref, e2_ref,
                 acc_ref, h_ref, *, tb):
    j = pl.program_id(1)

    @pl.when(j == 0)
    def _():
        x = x_ref[...]
        xn = x * lax.rsqrt(jnp.mean(x * x, axis=-1, keepdims=True) + RMS_EPS) * nw_ref[...]
        xnt_ref[...] = xn.T.astype(BF16)
        q_ref[...] = jnp.dot(wqt_ref[...], xnt_ref[...], preferred_element_type=F32).astype(BF16)
        acc_ref[...] = jnp.zeros_like(acc_ref)

        def head(h, carry):
            _peer_route(h, q_ref, keys_ref, sv1_ref, sv2_ref, cand_ref, thr1_ref, c1_ref, t2_ref, e2_ref, tb)
            return carry

        lax.fori_loop(0, PEER_HEADS, head, 0)

    xnt = xnt_ref[...]
    for ci in range(PEER_CHUNKS_PER_STEP):
        group = j * PEER_CHUNKS_PER_STEP + ci
        for k in range(PEER_EXPERT_CHUNK // PEER_SUB_CHUNK):
            e0 = ci * PEER_EXPERT_CHUNK + k * PEER_SUB_CHUNK
            a_sub = jnp.dot(u_ref[e0:e0 + PEER_SUB_CHUNK, :], xnt, preferred_element_type=F32)
            for sl in range(PEER_SUB_CHUNK // PEER_KEYS):
                i1l = k * (PEER_SUB_CHUNK // PEER_KEYS) + sl
                for r in range(PEER_KEYS // PEER_ROW_TILE):
                    rows = slice(r * PEER_ROW_TILE, (r + 1) * PEER_ROW_TILE)
                    off = sl * PEER_KEYS + r * PEER_ROW_TILE
                    w = jnp.zeros((PEER_ROW_TILE, tb), F32)
                    for h in range(PEER_HEADS):
                        thr = thr1_ref[h, group, i1l:i1l + 1, :]
                        c1 = c1_ref[h, group, i1l:i1l + 1, :]
                        w = w + jnp.where(t2_ref[h, rows, :] >= thr, c1 * e2_ref[h, rows, :], 0.0)
                    a = a_sub[off:off + PEER_ROW_TILE, :]
                    g = 0.5 * a * (1.0 + lax.erf(a * (1.0 / math.sqrt(2.0))))
                    h_ref[e0 + off:e0 + off + PEER_ROW_TILE, :] = (w * g).astype(BF16)
        erows = slice(ci * PEER_EXPERT_CHUNK, (ci + 1) * PEER_EXPERT_CHUNK)
        acc_ref[...] += jnp.dot(vt_ref[:, erows], h_ref[erows, :], preferred_element_type=F32)

    @pl.when(j == pl.num_programs(1) - 1)
    def _():
        y_ref[...] = x_ref[...] + acc_ref[...].T


def peer_block(x, norm_w, wqt_bf16, keys_bf16, u_bf16, vt_bf16):
    t, d = x.shape
    tb = _pick_tile(t, (512, 256, 128))
    step_experts = PEER_CHUNKS_PER_STEP * PEER_EXPERT_CHUNK
    n_groups = PEER_KEYS // PEER_I1_PER_CHUNK
    const = dict(pipeline_mode=pl.Buffered(1))
    scratch = [
        pltpu.VMEM((d, tb), BF16),
        pltpu.VMEM((PEER_HEADS * PEER_KEY_DIM, tb), BF16),
        pltpu.VMEM((PEER_RANK_ROWS, tb), F32),
        pltpu.VMEM((PEER_RANK_ROWS, tb), F32),
        pltpu.VMEM((PEER_RANKS * PEER_RANK_ROWS, tb), F32),
        pltpu.VMEM((PEER_HEADS, n_groups, PEER_I1_PER_CHUNK, tb), F32),
        pltpu.VMEM((PEER_HEADS, n_groups, PEER_I1_PER_CHUNK, tb), F32),
        pltpu.VMEM((PEER_HEADS, PEER_KEYS, tb), F32),
        pltpu.VMEM((PEER_HEADS, PEER_KEYS, tb), F32),
        pltpu.VMEM((d, tb), F32),
        pltpu.VMEM((step_experts, tb), BF16),
    ]
    return pl.pallas_call(
        functools.partial(_peer_kernel, tb=tb),
        grid=(t // tb, PEER_EXPERTS // step_experts),
        in_specs=[pl.BlockSpec((tb, d), lambda i, j: (i, 0)),
                  pl.BlockSpec((1, d), lambda i, j: (0, 0)),
                  pl.BlockSpec((PEER_HEADS * PEER_KEY_DIM, d), lambda i, j: (0, 0), **const),
                  pl.BlockSpec((2 * PEER_HEADS, PEER_KEYS, PEER_HALF), lambda i, j: (0, 0, 0), **const),
                  pl.BlockSpec((step_experts, d), lambda i, j: (j, 0)),
                  pl.BlockSpec((d, step_experts), lambda i, j: (0, j))],
        out_specs=pl.BlockSpec((tb, d), lambda i, j: (i, 0)),
        out_shape=jax.ShapeDtypeStruct((t, d), F32),
        scratch_shapes=scratch,
        compiler_params=pltpu.CompilerParams(
            dimension_semantics=("arbitrary", "arbitrary"), vmem_limit_bytes=VMEM_LIMIT_BYTES),
        name="peer_block",
    )(x, norm_w.reshape(1, d), wqt_bf16, keys_bf16, u_bf16, vt_bf16)


RWKV_VALUE_HALF = RWKV_HEAD_DIM // 2
RWKV_KEY_ROWS = 5
RWKV_TIME_TILE = 32
_ROW_W, _ROW_KK, _ROW_KKA, _ROW_K, _ROW_R = range(RWKV_KEY_ROWS)


def _tree_sum(parts):
    while len(parts) > 1:
        parts = [parts[i] + parts[i + 1] for i in range(0, len(parts) - 1, 2)] + (
            [parts[-1]] if len(parts) % 2 else [])
    return parts[0]


def _rwkv_scan_kernel(kv_ref, v_ref, s0_ref, o_ref, s_ref, *, tc):
    @pl.when(pl.program_id(1) == 0)
    def _():
        s_ref[...] = s0_ref[...]

    n_acc = 4

    def step(tt, carry):
        acc = [None] * n_acc
        for k in range(RWKV_HEAD_DIM):
            term = s_ref[k] * kv_ref[tt, _ROW_KK, k:k + 1, :]
            acc[k % n_acc] = term if acc[k % n_acc] is None else acc[k % n_acc] + term
        sa = -_tree_sum(acc)
        v = v_ref[tt]
        oacc = [None] * n_acc
        for k in range(RWKV_HEAD_DIM):
            s_new = (s_ref[k] * kv_ref[tt, _ROW_W, k:k + 1, :] + sa * kv_ref[tt, _ROW_KKA, k:k + 1, :]
                     + v * kv_ref[tt, _ROW_K, k:k + 1, :])
            s_ref[k] = s_new
            term = s_new * kv_ref[tt, _ROW_R, k:k + 1, :]
            oacc[k % n_acc] = term if oacc[k % n_acc] is None else oacc[k % n_acc] + term
        o_ref[tt] = _tree_sum(oacc)
        return carry

    lax.fori_loop(0, tc, step, 0)


def rwkv_scan(r, decay, k, v, kk, a, s0):
    b, l, h, n = r.shape
    lanes = b * h * 2
    vh = RWKV_VALUE_HALF

    def key_rows(x):
        x = jnp.transpose(x, (1, 3, 0, 2))
        return jnp.broadcast_to(x[..., None], (l, n, b, h, 2)).reshape(l, n, lanes)

    kv = jnp.stack([key_rows(decay), key_rows(kk), key_rows(kk * a), key_rows(k), key_rows(r)], axis=1)
    vv = jnp.transpose(v.reshape(b, l, h, 2, vh), (1, 4, 0, 2, 3)).reshape(l, vh, lanes)
    st = jnp.transpose(s0.reshape(b, h, 2, vh, n), (4, 3, 0, 1, 2)).reshape(n, vh, lanes)
    tc = _pick_tile(l, (RWKV_TIME_TILE,))
    o, s_fin = pl.pallas_call(
        functools.partial(_rwkv_scan_kernel, tc=tc),
        grid=(lanes // LANES, l // tc),
        in_specs=[pl.BlockSpec((tc, RWKV_KEY_ROWS, n, LANES), lambda g, t: (t, 0, 0, g)),
                  pl.BlockSpec((tc, vh, LANES), lambda g, t: (t, 0, g)),
                  pl.BlockSpec((n, vh, LANES), lambda g, t: (0, 0, g))],
        out_specs=[pl.BlockSpec((tc, vh, LANES), lambda g, t: (t, 0, g)),
                   pl.BlockSpec((n, vh, LANES), lambda g, t: (0, 0, g))],
        out_shape=[jax.ShapeDtypeStruct((l, vh, lanes), F32), jax.ShapeDtypeStruct((n, vh, lanes), F32)],
        compiler_params=pltpu.CompilerParams(
            dimension_semantics=("arbitrary", "arbitrary"), vmem_limit_bytes=VMEM_LIMIT_BYTES),
        name="rwkv_scan",
    )(kv, vv, st)
    o = jnp.transpose(o.reshape(l, vh, b, h, 2), (2, 0, 3, 4, 1)).reshape(b, l, h, n)
    s_fin = jnp.transpose(s_fin.reshape(n, vh, b, h, 2), (2, 3, 4, 1, 0)).reshape(b, h, n, n)
    return o, s_fin


HGRN_TIME_TILE = 128
HGRN_BLOCK = 16


def _block_cumsum(x, block):
    row = lax.broadcasted_iota(jnp.int32, x.shape, 0) % block
    sh = 1
    while sh < block:
        x = x + jnp.where(row >= sh, pltpu.roll(x, sh, 0), 0.0)
        sh *= 2
    return x


def _hgrn_kernel(q_ref, f_ref, i_ref, og_ref, lb_ref, nw_ref, s0_ref, o_ref, s_ref, st_ref):
    t_idx = pl.program_id(2)

    @pl.when(t_idx == 0)
    def _():
        st_ref[...] = s0_ref[...].T

    lb = lb_ref[...]
    fg = lb + (1.0 - lb) * jax.nn.sigmoid(f_ref[...])
    g = jnp.log(fg)
    kk = 1.0 - fg
    qr = q_ref[...]
    q = qr * jax.nn.sigmoid(qr)
    v = i_ref[...]
    gc_all = _block_cumsum(g, HGRN_BLOCK)
    ones = jnp.ones((HGRN_EXPAND, HGRN_EXPAND), BF16)
    srow = lax.broadcasted_iota(jnp.int32, (HGRN_BLOCK, HGRN_EXPAND), 0)
    st = st_ref[...]
    for blk in range(HGRN_TIME_TILE // HGRN_BLOCK):
        rows = slice(blk * HGRN_BLOCK, (blk + 1) * HGRN_BLOCK)
        qb, kb, vb, gc = q[rows], kk[rows], v[rows], gc_all[rows]
        pieces = []
        for t in range(HGRN_BLOCK):
            msk = srow <= t
            decay = jnp.where(msk, jnp.exp(jnp.where(msk, gc[t:t + 1, :] - gc, 0.0)), 0.0)
            pieces.append(decay * kb * qb[t:t + 1, :])
        y = jnp.concatenate(pieces, axis=0).astype(BF16)
        att = jnp.dot(y, ones, preferred_element_type=F32)
        att = att.reshape(HGRN_BLOCK, HGRN_BLOCK, HGRN_EXPAND)
        o_blk = jnp.sum(att * vb[None, :, :], axis=1)
        qe = (qb * jnp.exp(gc)).astype(BF16)
        o_blk = o_blk + lax.dot_general(qe, st.astype(BF16), (((1,), (1,)), ((), ())),
                                        preferred_element_type=F32)
        g_end = gc[HGRN_BLOCK - 1:HGRN_BLOCK, :]
        ke = (kb * jnp.exp(g_end - gc)).astype(BF16)
        st = st * jnp.exp(g_end) + lax.dot_general(vb.astype(BF16), ke, (((0,), (0,)), ((), ())),
                                                   preferred_element_type=F32)
        on = o_blk * lax.rsqrt(jnp.mean(o_blk * o_blk, axis=-1, keepdims=True) + RMS_EPS) * nw_ref[...]
        ogb = og_ref[rows, :]
        o_ref[rows, :] = on * (ogb * jax.nn.sigmoid(ogb))
    st_ref[...] = st

    @pl.when(t_idx == pl.num_programs(2) - 1)
    def _():
        s_ref[...] = st.T


def hgrn_mix(proj, s0, lb, norm_w):
    b, l, _ = proj.shape
    tl = HGRN_TIME_TILE
    c0 = RWKV_PROJ // LANES
    nh = HGRN_HEADS

    def col(section):
        return pl.BlockSpec((None, tl, HGRN_EXPAND), lambda bi, h, t, s=section: (bi, t, c0 + s * nh + h))

    return pl.pallas_call(
        _hgrn_kernel,
        grid=(b, nh, l // tl),
        in_specs=[col(0), col(1), col(2), col(3),
                  pl.BlockSpec((1, HGRN_EXPAND), lambda bi, h, t: (0, h)),
                  pl.BlockSpec((1, HGRN_EXPAND), lambda bi, h, t: (0, 0)),
                  pl.BlockSpec((None, None, HGRN_EXPAND, HGRN_EXPAND), lambda bi, h, t: (bi, h, 0, 0))],
        out_specs=[pl.BlockSpec((None, tl, HGRN_EXPAND), lambda bi, h, t: (bi, t, h)),
                   pl.BlockSpec((None, None, HGRN_EXPAND, HGRN_EXPAND), lambda bi, h, t: (bi, h, 0, 0))],
        out_shape=[jax.ShapeDtypeStruct((b, l, HGRN_WIDTH), F32),
                   jax.ShapeDtypeStruct((b, nh, HGRN_EXPAND, HGRN_EXPAND), F32)],
        scratch_shapes=[pltpu.VMEM((HGRN_EXPAND, HGRN_EXPAND), F32)],
        compiler_params=pltpu.CompilerParams(
            dimension_semantics=("arbitrary", "arbitrary", "arbitrary"), vmem_limit_bytes=VMEM_LIMIT_BYTES),
        name="hgrn_mix",
    )(proj, proj, proj, proj, lb.reshape(1, HGRN_WIDTH), norm_w.reshape(1, HGRN_EXPAND), s0)


SSD_TIME_TILE = 128
SSD_TAIL_ROW = SUBLANES - (CONV_WIDTH - 1)
SSD_HEAD_PAIR = 2 * SSM_HEAD_DIM


def _ssd_kernel(z_ref, x_ref, bc_ref, dt_ref, cbuf_ref, s0_ref, cw_ref, cb_ref, dtb_ref, alog_ref,
                dskip_ref, nw_ref, y_ref, nbuf_ref, sfin_ref, xpad_ref, st_ref):
    tl = SSD_TIME_TILE
    gn = SSM_GROUPS * SSM_STATE
    t_idx = pl.program_id(1)

    @pl.when(t_idx == 0)
    def _():
        xpad_ref[SSD_TAIL_ROW:SUBLANES, :] = cbuf_ref[...]
        st_ref[...] = s0_ref[...].reshape(SSM_INNER, SSM_STATE).T

    xpad_ref[SUBLANES:SUBLANES + tl, 0:SSM_INNER] = x_ref[...]
    xpad_ref[SUBLANES:SUBLANES + tl, SSM_INNER:CONV_DIM] = bc_ref[...]
    conv = cb_ref[...] + xpad_ref[SSD_TAIL_ROW:SSD_TAIL_ROW + tl, :] * cw_ref[0:1, :]
    for j in range(1, CONV_WIDTH):
        conv = conv + xpad_ref[SSD_TAIL_ROW + j:SSD_TAIL_ROW + j + tl, :] * cw_ref[j:j + 1, :]
    tail = xpad_ref[tl + SSD_TAIL_ROW:tl + SUBLANES, :]
    xpad_ref[SSD_TAIL_ROW:SUBLANES, :] = tail
    nbuf_ref[...] = tail
    act = conv * jax.nn.sigmoid(conv)
    xs = act[:, 0:SSM_INNER]
    bm = act[:, SSM_INNER:SSM_INNER + gn]
    cm = act[:, SSM_INNER + gn:CONV_DIM]

    dtv = dt_ref[...] + dtb_ref[...]
    dt = jnp.maximum(dtv, 0.0) + jnp.log(1.0 + jnp.exp(-jnp.abs(dtv)))
    acum = _block_cumsum(dt * (-jnp.exp(alog_ref[...])), tl)
    acum_t = acum.T
    dt_t = dt.T
    a_end = acum[tl - 1:tl, :]
    wdt = jnp.exp(a_end - acum) * dt
    e_end = jnp.exp(a_end)
    tri = (lax.broadcasted_iota(jnp.int32, (tl, tl), 1) <= lax.broadcasted_iota(jnp.int32, (tl, tl), 0))
    lo = lax.broadcasted_iota(jnp.int32, (1, SSD_HEAD_PAIR), 1) < SSM_HEAD_DIM

    for g in range(SSM_GROUPS):
        cm_g = cm[:, g * SSM_STATE:(g + 1) * SSM_STATE]
        bm_g = bm[:, g * SSM_STATE:(g + 1) * SSM_STATE]
        cb = lax.dot_general(cm_g.astype(BF16), bm_g.astype(BF16), (((1,), (1,)), ((), ())),
                             preferred_element_type=F32)
        for pp in range(SSM_HPG // 2):
            h0 = g * SSM_HPG + 2 * pp
            cols = slice(h0 * SSM_HEAD_DIM, h0 * SSM_HEAD_DIM + SSD_HEAD_PAIR)
            xs_pair = xs[:, cols]
            st_pair = st_ref[:, cols]
            y_pair = xs_pair * dskip_ref[:, cols]
            upd = jnp.zeros((SSM_STATE, SSD_HEAD_PAIR), F32)
            for h, keep in ((h0, lo), (h0 + 1, jnp.logical_not(lo))):
                col_a = jnp.broadcast_to(acum[:, h:h + 1], (tl, tl))
                decay = jnp.where(tri, jnp.exp(jnp.where(tri, col_a - acum_t[h:h + 1, :], 0.0)), 0.0)
                m = cb * decay * dt_t[h:h + 1, :]
                cme = cm_g * jnp.exp(col_a)
                lhs = jnp.concatenate([m, cme], axis=1).astype(BF16)
                xk = jnp.where(keep, xs_pair, 0.0)
                rhs = jnp.concatenate([xk, jnp.where(keep, st_pair, 0.0)], axis=0).astype(BF16)
                y_pair = y_pair + jnp.dot(lhs, rhs, preferred_element_type=F32)
                bms = (bm_g * jnp.broadcast_to(wdt[:, h:h + 1], (tl, SSM_STATE))).astype(BF16)
                upd = upd + lax.dot_general(bms, xk.astype(BF16), (((0,), (0,)), ((), ())),
                                            preferred_element_type=F32)
            dec = jnp.where(lo, jnp.broadcast_to(e_end[:, h0:h0 + 1], (1, SSD_HEAD_PAIR)),
                            jnp.broadcast_to(e_end[:, h0 + 1:h0 + 2], (1, SSD_HEAD_PAIR)))
            st_ref[:, cols] = st_pair * dec + upd
            y_ref[:, cols] = y_pair

    z = z_ref[...]
    y = y_ref[...] * (z * jax.nn.sigmoid(z))
    gw = SSM_INNER // SSM_GROUPS
    for g in range(SSM_GROUPS):
        yg = y[:, g * gw:(g + 1) * gw]
        y_ref[:, g * gw:(g + 1) * gw] = (yg * lax.rsqrt(jnp.mean(yg * yg, axis=-1, keepdims=True) + RMS_EPS)
                                        * nw_ref[:, g * gw:(g + 1) * gw])

    @pl.when(t_idx == pl.num_programs(1) - 1)
    def _():
        sfin_ref[...] = st_ref[...].T.reshape(SSM_HEADS, SSM_HEAD_DIM, SSM_STATE)


def ssd_mix(proj, conv_buf, s0, conv_w, conv_b, dt_bias, a_log, d_skip, norm_w):
    b, l, _ = proj.shape
    tl = SSD_TIME_TILE
    pad = LANES - SSM_HEADS
    row = lambda v, n: v.reshape(1, n)
    bc_w = CONV_DIM - SSM_INNER
    const = lambda shape: pl.BlockSpec(shape, lambda bi, t: (0,) * len(shape))
    return pl.pallas_call(
        _ssd_kernel,
        grid=(b, l // tl),
        in_specs=[pl.BlockSpec((None, tl, SSM_INNER), lambda bi, t: (bi, t, 0)),
                  pl.BlockSpec((None, tl, SSM_INNER), lambda bi, t: (bi, t, 1)),
                  pl.BlockSpec((None, tl, bc_w), lambda bi, t: (bi, t, 2 * SSM_INNER // bc_w)),
                  pl.BlockSpec((None, tl, LANES), lambda bi, t: (bi, t, (SSM_INNER + CONV_DIM) // LANES)),
                  pl.BlockSpec((None, CONV_WIDTH - 1, CONV_DIM), lambda bi, t: (bi, 0, 0)),
                  pl.BlockSpec((None, SSM_HEADS, SSM_HEAD_DIM, SSM_STATE), lambda bi, t: (bi, 0, 0, 0)),
                  const((CONV_WIDTH, CONV_DIM)), const((1, CONV_DIM)), const((1, LANES)), const((1, LANES)),
                  const((1, SSM_INNER)), const((1, SSM_INNER))],
        out_specs=[pl.BlockSpec((None, tl, SSM_INNER), lambda bi, t: (bi, t, 0)),
                   pl.BlockSpec((None, CONV_WIDTH - 1, CONV_DIM), lambda bi, t: (bi, 0, 0)),
                   pl.BlockSpec((None, SSM_HEADS, SSM_HEAD_DIM, SSM_STATE), lambda bi, t: (bi, 0, 0, 0))],
        out_shape=[jax.ShapeDtypeStruct((b, l, SSM_INNER), F32),
                   jax.ShapeDtypeStruct((b, CONV_WIDTH - 1, CONV_DIM), F32),
                   jax.ShapeDtypeStruct((b, SSM_HEADS, SSM_HEAD_DIM, SSM_STATE), F32)],
        scratch_shapes=[pltpu.VMEM((tl + SUBLANES, CONV_DIM), F32),
                        pltpu.VMEM((SSM_STATE, SSM_INNER), F32)],
        compiler_params=pltpu.CompilerParams(
            dimension_semantics=("arbitrary", "arbitrary"), vmem_limit_bytes=VMEM_LIMIT_BYTES),
        name="ssd_mix",
    )(proj, proj, proj, proj, conv_buf, s0, conv_w, row(conv_b, CONV_DIM),
      row(jnp.pad(dt_bias, (0, pad)), LANES), row(jnp.pad(a_log, (0, pad)), LANES),
      row(jnp.repeat(d_skip, SSM_HEAD_DIM), SSM_INNER), row(norm_w, SSM_INNER))


def _to_chunks(a, c):
    b, l = a.shape[:2]
    return jnp.swapaxes(a.reshape((b, l // c, c) + a.shape[2:]), 0, 1)


def _from_chunks(a):
    n, b, c = a.shape[:3]
    return jnp.swapaxes(a, 0, 1).reshape((b, n * c) + a.shape[3:])


def _masked_decay(diff, mask):
    return jnp.where(mask, jnp.exp(jnp.where(mask, diff, 0.0)), 0.0)


def _rwkv7_mixer(p, shift_prev, s0, mu, w0, w_up, a0, a_up, g_up, k_k, k_a, r_k, gn_w, gn_b):
    b, l, _ = p.shape
    prev = jnp.concatenate([shift_prev[:, None], p[:, :-1]], axis=1)
    m = p + mu * (prev - p)
    r, k, v, wd, ad, gd = jnp.split(m, RWKV_SPLITS, axis=-1)
    w_raw = w0 + jnp.tanh(wd) @ w_up
    decay = jnp.exp(-jnp.exp(-jax.nn.softplus(-w_raw) - 0.5))
    a = jax.nn.sigmoid(a0 + ad @ a_up)
    g = jax.nn.sigmoid(gd) @ g_up
    heads = lambda t: t.reshape(b, l, RWKV_HEADS, RWKV_HEAD_DIM)
    kk = heads(k * k_k)
    kk = kk / jnp.maximum(jnp.sqrt(jnp.sum(kk * kk, axis=-1, keepdims=True)), L2_EPS)
    k = heads(k * (1.0 + (a - 1.0) * k_a))
    r, v, a, decay = heads(r), heads(v), heads(a), heads(decay)

    o, s_fin = rwkv_scan(r, decay, k, v, kk, a, s0)
    mean = jnp.mean(o, axis=-1, keepdims=True)
    var = jnp.mean(jnp.square(o - mean), axis=-1, keepdims=True)
    o = ((o - mean) * lax.rsqrt(var + GN_EPS)).reshape(b, l, RWKV_WIDTH) * gn_w + gn_b
    bonus = jnp.sum(r * k * r_k, axis=-1, keepdims=True) * v
    out = (o + bonus.reshape(b, l, RWKV_WIDTH)) * g
    return out, p[:, -1], s_fin


def _chunked_gla(q, k, v, g, s0):
    c = math.gcd(q.shape[1], CHUNK)
    mask = jnp.tril(jnp.ones((c, c), dtype=bool))[None, :, :, None, None]

    def step(s, inp):
        qc, kc, vc, gc = inp
        gcum = jnp.cumsum(gc, axis=1)
        rel = _masked_decay(gcum[:, :, None] - gcum[:, None, :], mask)
        att = jnp.einsum('bthk,btshk->btsh', qc, rel * kc[:, None])
        o = (jnp.einsum('btsh,bshv->bthv', att, vc)
             + jnp.einsum('bthk,bhkv->bthv', qc * jnp.exp(gcum), s))
        g_end = gcum[:, -1]
        s = s * jnp.exp(g_end)[..., None] + jnp.einsum(
            'bshk,bshv->bhkv', kc * jnp.exp(g_end[:, None] - gcum), vc)
        return s, o

    s_fin, o = lax.scan(step, s0, tuple(_to_chunks(t, c) for t in (q, k, v, g)))
    return _from_chunks(o), s_fin


def _hgrn2_mixer(p, s0, lb, norm_w):
    b, l, _ = p.shape
    q, f, i, og = jnp.split(p, 4, axis=-1)
    fg = lb + (1.0 - lb) * jax.nn.sigmoid(f)
    log_f = jnp.log(fg)
    k = 1.0 - fg
    heads = lambda t: t.reshape(b, l, HGRN_HEADS, HGRN_EXPAND)
    o, s_fin = _chunked_gla(heads(jax.nn.silu(q)), heads(k), heads(i), heads(log_f), s0)
    o = o * lax.rsqrt(jnp.mean(o * o, axis=-1, keepdims=True) + RMS_EPS) * norm_w
    out = o.reshape(b, l, HGRN_WIDTH) * jax.nn.silu(og)
    return out, s_fin


def _chunked_ssd(x, bm, cm, log_a, s0):
    c = math.gcd(x.shape[1], CHUNK)
    mask = jnp.tril(jnp.ones((c, c), dtype=bool))[None, :, :, None, None]

    def step(s, inp):
        xc, bc, cc, ac = inp
        acum = jnp.cumsum(ac, axis=1)
        lmat = _masked_decay(acum[:, :, None] - acum[:, None, :], mask)
        cb = jnp.einsum('btgn,bsgn->btsg', cc, bc)
        y = jnp.einsum('btsgr,bsgrp->btgrp', cb[..., None] * lmat, xc)
        y = y + jnp.einsum('btgn,bgrpn->btgrp', cc, s) * jnp.exp(acum)[..., None]
        a_end = acum[:, -1]
        s = s * jnp.exp(a_end)[..., None, None] + jnp.einsum(
            'bsgn,bsgrp->bgrpn', bc, xc * jnp.exp(a_end[:, None] - acum)[..., None])
        return s, y

    s_fin, y = lax.scan(step, s0, tuple(_to_chunks(t, c) for t in (x, bm, cm, log_a)))
    return _from_chunks(y), s_fin


def _mamba2_mixer(p, conv_buf, s0, conv_w, conv_b, dt_bias, a_log, d_skip, norm_w):
    b, l, _ = p.shape
    z, xbc, dt = jnp.split(p, [SSM_INNER, SSM_INNER + CONV_DIM], axis=-1)
    xpad = jnp.concatenate([conv_buf, xbc], axis=1)
    conv = conv_b + xpad[:, 0:l] * conv_w[0]
    for j in range(1, CONV_WIDTH):
        conv = conv + xpad[:, j:j + l] * conv_w[j]
    xbc = jax.nn.silu(conv)
    new_buf = xpad[:, -(CONV_WIDTH - 1):]
    xs, bm, cm = jnp.split(xbc, [SSM_INNER, SSM_INNER + SSM_GROUPS * SSM_STATE], axis=-1)
    xs = xs.reshape(b, l, SSM_GROUPS, SSM_HPG, SSM_HEAD_DIM)
    bm = bm.reshape(b, l, SSM_GROUPS, SSM_STATE)
    cm = cm.reshape(b, l, SSM_GROUPS, SSM_STATE)
    dt = jax.nn.softplus(dt + dt_bias).reshape(b, l, SSM_GROUPS, SSM_HPG)
    log_a = dt * (-jnp.exp(a_log)).reshape(SSM_GROUPS, SSM_HPG)
    s0 = s0.reshape(b, SSM_GROUPS, SSM_HPG, SSM_HEAD_DIM, SSM_STATE)
    y, s_fin = _chunked_ssd(xs * dt[..., None], bm, cm, log_a, s0)
    y = y + xs * d_skip.reshape(SSM_GROUPS, SSM_HPG, 1)
    y = y.reshape(b, l, SSM_INNER) * jax.nn.silu(z)
    yg = y.reshape(b, l, SSM_GROUPS, SSM_INNER // SSM_GROUPS)
    yg = yg * lax.rsqrt(jnp.mean(yg * yg, axis=-1, keepdims=True) + RMS_EPS)
    out = yg.reshape(b, l, SSM_INNER) * norm_w
    return out, new_buf, s_fin.reshape(b, SSM_HEADS, SSM_HEAD_DIM, SSM_STATE)


def _prepare_weights(w):
    ssm_pad = (-SSM_PROJ) % (6 * LANES)
    return dict(
        ab_w_in=w['ab_w_in'].astype(BF16),
        ab_w_out=w['ab_w_out'].astype(BF16),
        ssm_w_in=jnp.pad(w['ssm_w_in'], ((0, 0), (0, 0), (0, ssm_pad))).astype(BF16),
        ssm_w_out=w['ssm_w_out'].astype(BF16),
        peer_wqt=jnp.swapaxes(w['peer_w_q'], 1, 2).astype(BF16),
        peer_keys=w['peer_sub_keys'].reshape(DEPTH, 2 * PEER_HEADS, PEER_KEYS, PEER_HALF).astype(BF16),
        peer_u=w['peer_u'].astype(BF16),
        peer_vt=jnp.swapaxes(w['peer_v'], 1, 2).astype(BF16),
    )


def _run_trunk(x, st_shift, st_rwkv, st_hgrn, st_ssm, st_conv, w, wp):
    b, l, d = x.shape
    t = b * l
    lb = jax.nn.softmax(w['hgrn_lower_bounds'], axis=0)
    lb = jnp.cumsum(lb, axis=0) - lb[0]
    n_shift, n_rwkv, n_hgrn, n_ssm, n_conv = [], [], [], [], []
    xt = x.reshape(t, d)
    for i in range(DEPTH):
        j = i // 2
        if i % 2 == 0:
            proj = norm_matmul(xt, w['norm_mix'][i], wp['ab_w_in'][j]).reshape(b, l, AB_PROJ)
            o_a, sh, sa = _rwkv7_mixer(proj[..., :RWKV_PROJ], st_shift[j], st_rwkv[j],
                                       w['rwkv_mu'][j], w['rwkv_w0'][j], w['rwkv_w_up'][j], w['rwkv_a0'][j],
                                       w['rwkv_a_up'][j], w['rwkv_g_up'][j], w['rwkv_k_k'][j], w['rwkv_k_a'][j],
                                       w['rwkv_r_k'][j], w['rwkv_gn_w'][j], w['rwkv_gn_b'][j])
            if l % HGRN_TIME_TILE == 0:
                o_b, sb = hgrn_mix(proj, st_hgrn[j], lb[j], w['hgrn_norm_w'][j])
            else:
                o_b, sb = _hgrn2_mixer(proj[..., RWKV_PROJ:], st_hgrn[j], lb[j], w['hgrn_norm_w'][j])
            mix_in = jnp.concatenate([o_a, o_b], axis=-1).reshape(t, MIX_WIDTH)
            xt = matmul_residual(mix_in, wp['ab_w_out'][j], xt)
            n_shift.append(sh); n_rwkv.append(sa); n_hgrn.append(sb)
        else:
            proj = norm_matmul(xt, w['norm_mix'][i], wp['ssm_w_in'][j])
            ssm_w = (w['ssm_conv_w'][j], w['ssm_conv_b'][j], w['ssm_dt_bias'][j], w['ssm_a_log'][j],
                     w['ssm_d'][j], w['ssm_norm_w'][j])
            if l % SSD_TIME_TILE == 0:
                o_c, cb, sc = ssd_mix(proj.reshape(b, l, -1), st_conv[j], st_ssm[j], *ssm_w)
            else:
                o_c, cb, sc = _mamba2_mixer(proj[:, :SSM_PROJ].reshape(b, l, SSM_PROJ), st_conv[j], st_ssm[j], *ssm_w)
            xt = matmul_residual(o_c.reshape(t, SSM_INNER), wp['ssm_w_out'][j], xt)
            n_conv.append(cb); n_ssm.append(sc)
        xt = peer_block(xt, w['norm_ffn'][i], wp['peer_wqt'][i], wp['peer_keys'][i],
                        wp['peer_u'][i], wp['peer_vt'][i])
    y = rms_norm_rows(xt, w['norm_final']).reshape(b, l, d)
    return (y, jnp.stack(n_shift), jnp.stack(n_rwkv), jnp.stack(n_hgrn), jnp.stack(n_ssm), jnp.stack(n_conv))


def kernel(x_prompt, x_sample, state_rwkv_shift, state_rwkv, state_hgrn, state_ssm, state_conv,
           norm_mix, norm_ffn, norm_final, ab_w_in, ab_w_out, rwkv_mu, rwkv_w0, rwkv_w_up, rwkv_a0,
           rwkv_a_up, rwkv_g_up, rwkv_k_k, rwkv_k_a, rwkv_r_k, rwkv_gn_w, rwkv_gn_b, hgrn_lower_bounds,
           hgrn_norm_w, ssm_w_in, ssm_conv_w, ssm_conv_b, ssm_dt_bias, ssm_a_log, ssm_d, ssm_norm_w,
           ssm_w_out, peer_w_q, peer_sub_keys, peer_u, peer_v):
    w = dict(norm_mix=norm_mix, norm_ffn=norm_ffn, norm_final=norm_final, ab_w_in=ab_w_in, ab_w_out=ab_w_out,
             rwkv_mu=rwkv_mu, rwkv_w0=rwkv_w0, rwkv_w_up=rwkv_w_up, rwkv_a0=rwkv_a0, rwkv_a_up=rwkv_a_up,
             rwkv_g_up=rwkv_g_up, rwkv_k_k=rwkv_k_k, rwkv_k_a=rwkv_k_a, rwkv_r_k=rwkv_r_k, rwkv_gn_w=rwkv_gn_w,
             rwkv_gn_b=rwkv_gn_b, hgrn_lower_bounds=hgrn_lower_bounds, hgrn_norm_w=hgrn_norm_w,
             ssm_w_in=ssm_w_in, ssm_conv_w=ssm_conv_w, ssm_conv_b=ssm_conv_b, ssm_dt_bias=ssm_dt_bias,
             ssm_a_log=ssm_a_log, ssm_d=ssm_d, ssm_norm_w=ssm_norm_w, ssm_w_out=ssm_w_out,
             peer_w_q=peer_w_q, peer_sub_keys=peer_sub_keys, peer_u=peer_u, peer_v=peer_v)
    wp = _prepare_weights(w)
    bp = x_prompt.shape[0]
    prompt = _run_trunk(
        x_prompt,
        jnp.zeros((DEPTH // 2, bp, RWKV_PROJ), F32),
        jnp.zeros((DEPTH // 2, bp, RWKV_HEADS, RWKV_HEAD_DIM, RWKV_HEAD_DIM), F32),
        jnp.zeros((DEPTH // 2, bp, HGRN_HEADS, HGRN_EXPAND, HGRN_EXPAND), F32),
        jnp.zeros((DEPTH // 2, bp, SSM_HEADS, SSM_HEAD_DIM, SSM_STATE), F32),
        jnp.zeros((DEPTH // 2, bp, CONV_WIDTH - 1, CONV_DIM), F32),
        w, wp)
    sample = _run_trunk(x_sample, state_rwkv_shift, state_rwkv, state_hgrn, state_ssm, state_conv, w, wp)
    return (prompt[0], sample[0]) + prompt[1:] + sample[1:]
```

```python
import functools
import math

import jax
import jax.numpy as jnp
from jax import lax
from jax.experimental import pallas as pl
from jax.experimental.pallas import tpu as pltpu

D_MODEL = 1024
DEPTH = 4
MIX_WIDTH = D_MODEL
RWKV_WIDTH = MIX_WIDTH // 2
RWKV_HEAD_DIM = 64
RWKV_HEADS = RWKV_WIDTH // RWKV_HEAD_DIM
DECAY_LORA = 64
AAA_LORA = 64
GATE_LORA = 128
RWKV_SPLITS = (RWKV_WIDTH, 2 * RWKV_WIDTH, 3 * RWKV_WIDTH, 3 * RWKV_WIDTH + DECAY_LORA,
               3 * RWKV_WIDTH + DECAY_LORA + AAA_LORA)
RWKV_PROJ = 3 * RWKV_WIDTH + DECAY_LORA + AAA_LORA + GATE_LORA
HGRN_WIDTH = MIX_WIDTH - RWKV_WIDTH
HGRN_EXPAND = 128
HGRN_HEADS = HGRN_WIDTH // HGRN_EXPAND
HGRN_PROJ = 4 * HGRN_WIDTH
AB_PROJ = RWKV_PROJ + HGRN_PROJ
SSM_INNER = 2 * D_MODEL
SSM_HEAD_DIM = 64
SSM_HEADS = SSM_INNER // SSM_HEAD_DIM
SSM_STATE = 128
SSM_GROUPS = 4
SSM_HPG = SSM_HEADS // SSM_GROUPS
CONV_WIDTH = 4
CONV_DIM = SSM_INNER + 2 * SSM_GROUPS * SSM_STATE
SSM_PROJ = SSM_INNER + CONV_DIM + SSM_HEADS
CHUNK = 64
PEER_HEADS = 8
PEER_KEYS = 128
PEER_EXPERTS = PEER_KEYS * PEER_KEYS
PEER_KEY_DIM = 256
PEER_HALF = PEER_KEY_DIM // 2
PEER_TOPK = 16
RMS_EPS = 1e-6
GN_EPS = 64e-5
L2_EPS = 1e-12
F32 = jnp.float32
BF16 = jnp.bfloat16

LANES = 128
SUBLANES = 8
VMEM_LIMIT_BYTES = 56 * 1024 * 1024

PEER_I1_PER_CHUNK = 8
PEER_EXPERT_CHUNK = PEER_I1_PER_CHUNK * PEER_KEYS
PEER_ROW_TILE = 32
PEER_SUB_CHUNK = 2 * PEER_KEYS
PEER_CHUNKS_PER_STEP = 2
PEER_RANKS = PEER_TOPK + 1
PEER_RANK_ROWS = 24
MASKED = -1e30


def _pick_tile(n, candidates):
    for c in candidates:
        if n % c == 0:
            return c
    return n


def _norm_matmul_kernel(x_ref, nw_ref, w_ref, o_ref):
    x = x_ref[...]
    xn = x * lax.rsqrt(jnp.mean(x * x, axis=-1, keepdims=True) + RMS_EPS) * nw_ref[...]
    o_ref[...] = jnp.dot(xn.astype(BF16), w_ref[...], preferred_element_type=F32)


def norm_matmul(x, norm_w, w_bf16):
    m, k = x.shape
    n = w_bf16.shape[1]
    tm = _pick_tile(m, (512, 256, 128))
    tn = _pick_tile(n, (1024, 896, 768, 640, 512, 384, 256, 128))
    return pl.pallas_call(
        _norm_matmul_kernel,
        grid=(m // tm, n // tn),
        in_specs=[pl.BlockSpec((tm, k), lambda i, j: (i, 0)),
                  pl.BlockSpec((1, k), lambda i, j: (0, 0)),
                  pl.BlockSpec((k, tn), lambda i, j: (0, j))],
        out_specs=pl.BlockSpec((tm, tn), lambda i, j: (i, j)),
        out_shape=jax.ShapeDtypeStruct((m, n), F32),
        compiler_params=pltpu.CompilerParams(
            dimension_semantics=("arbitrary", "arbitrary"), vmem_limit_bytes=VMEM_LIMIT_BYTES),
        name="norm_matmul",
    )(x, norm_w.reshape(1, k), w_bf16)


def _matmul_residual_kernel(a_ref, w_ref, r_ref, o_ref):
    o_ref[...] = r_ref[...] + jnp.dot(a_ref[...].astype(BF16), w_ref[...], preferred_element_type=F32)


def matmul_residual(a, w_bf16, res):
    m, k = a.shape
    n = w_bf16.shape[1]
    tm = _pick_tile(m, (512, 256, 128))
    tn = _pick_tile(n, (1024, 512, 256, 128))
    return pl.pallas_call(
        _matmul_residual_kernel,
        grid=(m // tm, n // tn),
        in_specs=[pl.BlockSpec((tm, k), lambda i, j: (i, 0)),
                  pl.BlockSpec((k, tn), lambda i, j: (0, j)),
                  pl.BlockSpec((tm, tn), lambda i, j: (i, j))],
        out_specs=pl.BlockSpec((tm, tn), lambda i, j: (i, j)),
        out_shape=jax.ShapeDtypeStruct((m, n), F32),
        compiler_params=pltpu.CompilerParams(
            dimension_semantics=("arbitrary", "arbitrary"), vmem_limit_bytes=VMEM_LIMIT_BYTES),
        name="matmul_residual",
    )(a, w_bf16, res)


def _rms_norm_kernel(x_ref, nw_ref, o_ref):
    x = x_ref[...]
    o_ref[...] = x * lax.rsqrt(jnp.mean(x * x, axis=-1, keepdims=True) + RMS_EPS) * nw_ref[...]


def rms_norm_rows(x, norm_w):
    m, k = x.shape
    tm = _pick_tile(m, (512, 256, 128))
    return pl.pallas_call(
        _rms_norm_kernel,
        grid=(m // tm,),
        in_specs=[pl.BlockSpec((tm, k), lambda i: (i, 0)), pl.BlockSpec((1, k), lambda i: (0, 0))],
        out_specs=pl.BlockSpec((tm, k), lambda i: (i, 0)),
        out_shape=jax.ShapeDtypeStruct((m, k), F32),
        compiler_params=pltpu.CompilerParams(dimension_semantics=("arbitrary",)),
        name="rms_norm",
    )(x, norm_w.reshape(1, k))


def _top_values(s, sv_ref):
    work = s
    rank = jnp.full(s.shape, float(PEER_RANKS), F32)
    for k in range(PEER_RANKS):
        m = jnp.max(work, axis=0, keepdims=True)
        sv_ref[k:k + 1, :] = m
        hit = work == m
        rank = jnp.where(hit, float(k), rank)
        work = jnp.where(hit, MASKED, work)
    return rank


def _peer_route(h, q_ref, keys_ref, sv1_ref, sv2_ref, cand_ref, n1_ref, c1_ref, rank2_ref, e2_ref, tb):
    q1 = q_ref[pl.ds(pl.multiple_of(h * PEER_KEY_DIM, PEER_KEY_DIM), PEER_HALF), :]
    q2 = q_ref[pl.ds(pl.multiple_of(h * PEER_KEY_DIM + PEER_HALF, PEER_HALF), PEER_HALF), :]
    s1 = jnp.dot(keys_ref[h], q1, preferred_element_type=F32)
    s2 = jnp.dot(keys_ref[PEER_HEADS + h], q2, preferred_element_type=F32)
    sv2_ref[PEER_RANKS:, :] = jnp.full((PEER_RANK_ROWS - PEER_RANKS, tb), MASKED, F32)
    _top_values(s1, sv1_ref)
    rank2 = _top_values(s2, sv2_ref)
    max1 = sv1_ref[0:1, :]
    max2 = sv2_ref[0:1, :]
    t1 = s1 - max1
    t2 = s2 - max2
    tv2 = jnp.where(sv2_ref[...] > 0.5 * MASKED, sv2_ref[...] - max2, MASKED)
    cand_ref[0:PEER_RANK_ROWS, :] = tv2
    for a in range(1, PEER_RANKS):
        r0 = PEER_RANK_ROWS + (a - 1) * SUBLANES
        cand_ref[r0:r0 + SUBLANES, :] = (sv1_ref[a:a + 1, :] - max1) + tv2[0:SUBLANES, :]
    work = cand_ref[...]
    z = jnp.zeros((1, tb), F32)
    m_prev = None
    m = None
    for k in range(PEER_RANKS):
        m_prev = m
        m = jnp.max(work, axis=0, keepdims=True)
        if k < PEER_TOPK:
            z = z + jnp.exp(m)
            work = jnp.where(work == m, MASKED, work)
    thr = 0.5 * (m_prev + m)
    thr1 = thr - t1
    n1 = jnp.zeros(t1.shape, F32)
    for b in range(PEER_TOPK):
        n1 = n1 + jnp.where(tv2[b:b + 1, :] >= thr1, 1.0, 0.0)
    n1_ref[h] = n1.reshape(n1_ref.shape[1:])
    c1_ref[h] = (jnp.exp(t1) / z).reshape(c1_ref.shape[1:])
    rank2_ref[h] = rank2.astype(BF16)
    e2_ref[h] = jnp.exp(t2).astype(BF16)


def _peer_kernel(x_ref, nw_ref, wqt_ref, keys_ref, u_ref, vt_ref, y_ref,
                 xnt_ref, q_ref, sv1_ref, sv2_ref, cand_ref, n1_ref, c1_ref, rank2_ref, e2_ref,
                 acc_ref, h_ref, *, tb):
    j = pl.program_id(1)

    @pl.when(j == 0)
    def _():
        x = x_ref[...]
        xn = x * lax.rsqrt(jnp.mean(x * x, axis=-1, keepdims=True) + RMS_EPS) * nw_ref[...]
        xnt_ref[...] = xn.T.astype(BF16)
        q_ref[...] = jnp.dot(wqt_ref[...], xnt_ref[...], preferred_element_type=F32).astype(BF16)
        acc_ref[...] = jnp.zeros_like(acc_ref)

        def head(h, carry):
            _peer_route(h, q_ref, keys_ref, sv1_ref, sv2_ref, cand_ref, n1_ref, c1_ref, rank2_ref, e2_ref, tb)
            return carry

        lax.fori_loop(0, PEER_HEADS, head, 0)

    xnt = xnt_ref[...]
    for ci in range(PEER_CHUNKS_PER_STEP):
        group = j * PEER_CHUNKS_PER_STEP + ci
        for k in range(PEER_EXPERT_CHUNK // PEER_SUB_CHUNK):
            e0 = ci * PEER_EXPERT_CHUNK + k * PEER_SUB_CHUNK
            a_sub = jnp.dot(u_ref[e0:e0 + PEER_SUB_CHUNK, :], xnt, preferred_element_type=F32)
            for sl in range(PEER_SUB_CHUNK // PEER_KEYS):
                i1l = k * (PEER_SUB_CHUNK // PEER_KEYS) + sl
                n_tiles = PEER_KEYS // PEER_ROW_TILE
                w = [jnp.zeros((PEER_ROW_TILE, tb), BF16)] * n_tiles
                for h in range(PEER_HEADS):
                    n1 = jnp.broadcast_to(n1_ref[h, group, i1l:i1l + 1, :], (PEER_ROW_TILE, tb)).astype(BF16)
                    c1 = jnp.broadcast_to(c1_ref[h, group, i1l:i1l + 1, :], (PEER_ROW_TILE, tb)).astype(BF16)
                    for r in range(n_tiles):
                        rows = slice(r * PEER_ROW_TILE, (r + 1) * PEER_ROW_TILE)
                        w[r] = w[r] + jnp.where(rank2_ref[h, rows, :] < n1, c1 * e2_ref[h, rows, :],
                                                jnp.zeros((), BF16))
                for r in range(n_tiles):
                    off = sl * PEER_KEYS + r * PEER_ROW_TILE
                    a = a_sub[off:off + PEER_ROW_TILE, :]
                    g = 0.5 * a * (1.0 + lax.erf(a * (1.0 / math.sqrt(2.0))))
                    h_ref[e0 + off:e0 + off + PEER_ROW_TILE, :] = w[r] * g.astype(BF16)
        erows = slice(ci * PEER_EXPERT_CHUNK, (ci + 1) * PEER_EXPERT_CHUNK)
        acc_ref[...] += jnp.dot(vt_ref[:, erows], h_ref[erows, :], preferred_element_type=F32)

    @pl.when(j == pl.num_programs(1) - 1)
    def _():
        y_ref[...] = x_ref[...] + acc_ref[...].T


def peer_block(x, norm_w, wqt_bf16, keys_bf16, u_bf16, vt_bf16):
    t, d = x.shape
    tb = _pick_tile(t, (512, 256, 128))
    step_experts = PEER_CHUNKS_PER_STEP * PEER_EXPERT_CHUNK
    n_groups = PEER_KEYS // PEER_I1_PER_CHUNK
    const = dict(pipeline_mode=pl.Buffered(1))
    scratch = [
        pltpu.VMEM((d, tb), BF16),
        pltpu.VMEM((PEER_HEADS * PEER_KEY_DIM, tb), BF16),
        pltpu.VMEM((PEER_RANK_ROWS, tb), F32),
        pltpu.VMEM((PEER_RANK_ROWS, tb), F32),
        pltpu.VMEM((PEER_RANK_ROWS + (PEER_RANKS - 1) * SUBLANES, tb), F32),
        pltpu.VMEM((PEER_HEADS, n_groups, PEER_I1_PER_CHUNK, tb), F32),
        pltpu.VMEM((PEER_HEADS, n_groups, PEER_I1_PER_CHUNK, tb), F32),
        pltpu.VMEM((PEER_HEADS, PEER_KEYS, tb), BF16),
        pltpu.VMEM((PEER_HEADS, PEER_KEYS, tb), BF16),
        pltpu.VMEM((d, tb), F32),
        pltpu.VMEM((step_experts, tb), BF16),
    ]
    return pl.pallas_call(
        functools.partial(_peer_kernel, tb=tb),
        grid=(t // tb, PEER_EXPERTS // step_experts),
        in_specs=[pl.BlockSpec((tb, d), lambda i, j: (i, 0)),
                  pl.BlockSpec((1, d), lambda i, j: (0, 0)),
                  pl.BlockSpec((PEER_HEADS * PEER_KEY_DIM, d), lambda i, j: (0, 0), **const),
                  pl.BlockSpec((2 * PEER_HEADS, PEER_KEYS, PEER_HALF), lambda i, j: (0, 0, 0), **const),
                  pl.BlockSpec((step_experts, d), lambda i, j: (j, 0)),
                  pl.BlockSpec((d, step_experts), lambda i, j: (0, j))],
        out_specs=pl.BlockSpec((tb, d), lambda i, j: (i, 0)),
        out_shape=jax.ShapeDtypeStruct((t, d), F32),
        scratch_shapes=scratch,
        compiler_params=pltpu.CompilerParams(
            dimension_semantics=("arbitrary", "arbitrary"), vmem_limit_bytes=VMEM_LIMIT_BYTES),
        name="peer_block",
    )(x, norm_w.reshape(1, d), wqt_bf16, keys_bf16, u_bf16, vt_bf16)


RWKV_VALUE_HALF = RWKV_HEAD_DIM // 2
RWKV_TIME_TILE = 32


def _tree_sum(parts):
    while len(parts) > 1:
        parts = [parts[i] + parts[i + 1] for i in range(0, len(parts) - 1, 2)] + (
            [parts[-1]] if len(parts) % 2 else [])
    return parts[0]


def _rwkv_scan_kernel(w_ref, kk_ref, kka_ref, k_ref, r_ref, v_ref, s0_ref, o_ref, s_ref, *, tc):
    @pl.when(pl.program_id(1) == 0)
    def _():
        s_ref[...] = s0_ref[...]

    n_acc = 4

    def step(tt, carry):
        acc = [None] * n_acc
        for k in range(RWKV_HEAD_DIM):
            term = s_ref[k] * kk_ref[tt, k:k + 1, :]
            acc[k % n_acc] = term if acc[k % n_acc] is None else acc[k % n_acc] + term
        sa = -_tree_sum(acc)
        v = v_ref[tt]
        oacc = [None] * n_acc
        for k in range(RWKV_HEAD_DIM):
            s_new = (s_ref[k] * w_ref[tt, k:k + 1, :] + sa * kka_ref[tt, k:k + 1, :]
                     + v * k_ref[tt, k:k + 1, :])
            s_ref[k] = s_new
            term = s_new * r_ref[tt, k:k + 1, :]
            oacc[k % n_acc] = term if oacc[k % n_acc] is None else oacc[k % n_acc] + term
        o_ref[tt] = _tree_sum(oacc)
        return carry

    lax.fori_loop(0, tc, step, 0)


def rwkv_scan(r, decay, k, v, kk, a, s0):
    b, l, h, n = r.shape
    lanes = b * h * 2
    vh = RWKV_VALUE_HALF

    def key_rows(x):
        x = jnp.transpose(x, (1, 3, 0, 2))
        return jnp.broadcast_to(x[..., None], (l, n, b, h, 2)).reshape(l, n, lanes)

    keyed = [key_rows(decay), key_rows(kk), key_rows(kk * a), key_rows(k), key_rows(r)]
    vv = jnp.transpose(v.reshape(b, l, h, 2, vh), (1, 4, 0, 2, 3)).reshape(l, vh, lanes)
    st = jnp.transpose(s0.reshape(b, h, 2, vh, n), (4, 3, 0, 1, 2)).reshape(n, vh, lanes)
    tc = _pick_tile(l, (RWKV_TIME_TILE,))
    o, s_fin = pl.pallas_call(
        functools.partial(_rwkv_scan_kernel, tc=tc),
        grid=(lanes // LANES, l // tc),
        in_specs=[pl.BlockSpec((tc, n, LANES), lambda g, t: (t, 0, g))] * len(keyed) + [
                  pl.BlockSpec((tc, vh, LANES), lambda g, t: (t, 0, g)),
                  pl.BlockSpec((n, vh, LANES), lambda g, t: (0, 0, g))],
        out_specs=[pl.BlockSpec((tc, vh, LANES), lambda g, t: (t, 0, g)),
                   pl.BlockSpec((n, vh, LANES), lambda g, t: (0, 0, g))],
        out_shape=[jax.ShapeDtypeStruct((l, vh, lanes), F32), jax.ShapeDtypeStruct((n, vh, lanes), F32)],
        compiler_params=pltpu.CompilerParams(
            dimension_semantics=("arbitrary", "arbitrary"), vmem_limit_bytes=VMEM_LIMIT_BYTES),
        name="rwkv_scan",
    )(*keyed, vv, st)
    o = jnp.transpose(o.reshape(l, vh, b, h, 2), (2, 0, 3, 4, 1)).reshape(b, l, h, n)
    s_fin = jnp.transpose(s_fin.reshape(n, vh, b, h, 2), (2, 3, 4, 1, 0)).reshape(b, h, n, n)
    return o, s_fin


HGRN_TIME_TILE = 128
HGRN_BLOCK = 16


def _block_cumsum(x, block):
    row = lax.broadcasted_iota(jnp.int32, x.shape, 0) % block
    sh = 1
    while sh < block:
        x = x + jnp.where(row >= sh, pltpu.roll(x, sh, 0), 0.0)
        sh *= 2
    return x


def _hgrn_kernel(q_ref, f_ref, i_ref, og_ref, lb_ref, nw_ref, s0_ref, o_ref, s_ref, st_ref):
    t_idx = pl.program_id(2)

    @pl.when(t_idx == 0)
    def _():
        st_ref[...] = s0_ref[...].T

    lb = lb_ref[...]
    fg = lb + (1.0 - lb) * jax.nn.sigmoid(f_ref[...])
    g = jnp.log(fg)
    kk = 1.0 - fg
    qr = q_ref[...]
    q = qr * jax.nn.sigmoid(qr)
    v = i_ref[...]
    gc_all = _block_cumsum(g, HGRN_BLOCK)
    ones = jnp.ones((HGRN_EXPAND, HGRN_EXPAND), BF16)
    srow = lax.broadcasted_iota(jnp.int32, (HGRN_BLOCK, HGRN_EXPAND), 0)
    st = st_ref[...]
    for blk in range(HGRN_TIME_TILE // HGRN_BLOCK):
        rows = slice(blk * HGRN_BLOCK, (blk + 1) * HGRN_BLOCK)
        qb, kb, vb, gc = q[rows], kk[rows], v[rows], gc_all[rows]
        pieces = []
        for t in range(HGRN_BLOCK):
            msk = srow <= t
            decay = jnp.where(msk, jnp.exp(jnp.where(msk, gc[t:t + 1, :] - gc, 0.0)), 0.0)
            pieces.append(decay * kb * qb[t:t + 1, :])
        y = jnp.concatenate(pieces, axis=0).astype(BF16)
        att = jnp.dot(y, ones, preferred_element_type=F32)
        att = att.reshape(HGRN_BLOCK, HGRN_BLOCK, HGRN_EXPAND)
        o_blk = jnp.sum(att * vb[None, :, :], axis=1)
        qe = (qb * jnp.exp(gc)).astype(BF16)
        o_blk = o_blk + lax.dot_general(qe, st.astype(BF16), (((1,), (1,)), ((), ())),
                                        preferred_element_type=F32)
        g_end = gc[HGRN_BLOCK - 1:HGRN_BLOCK, :]
        ke = (kb * jnp.exp(g_end - gc)).astype(BF16)
        st = st * jnp.exp(g_end) + lax.dot_general(vb.astype(BF16), ke, (((0,), (0,)), ((), ())),
                                                   preferred_element_type=F32)
        on = o_blk * lax.rsqrt(jnp.mean(o_blk * o_blk, axis=-1, keepdims=True) + RMS_EPS) * nw_ref[...]
        ogb = og_ref[rows, :]
        o_ref[rows, :] = on * (ogb * jax.nn.sigmoid(ogb))
    st_ref[...] = st

    @pl.when(t_idx == pl.num_programs(2) - 1)
    def _():
        s_ref[...] = st.T


def hgrn_mix(proj, s0, lb, norm_w):
    b, l, _ = proj.shape
    tl = HGRN_TIME_TILE
    c0 = RWKV_PROJ // LANES
    nh = HGRN_HEADS

    def col(section):
        return pl.BlockSpec((None, tl, HGRN_EXPAND), lambda bi, h, t, s=section: (bi, t, c0 + s * nh + h))

    return pl.pallas_call(
        _hgrn_kernel,
        grid=(b, nh, l // tl),
        in_specs=[col(0), col(1), col(2), col(3),
                  pl.BlockSpec((1, HGRN_EXPAND), lambda bi, h, t: (0, h)),
                  pl.BlockSpec((1, HGRN_EXPAND), lambda bi, h, t: (0, 0)),
                  pl.BlockSpec((None, None, HGRN_EXPAND, HGRN_EXPAND), lambda bi, h, t: (bi, h, 0, 0))],
        out_specs=[pl.BlockSpec((None, tl, HGRN_EXPAND), lambda bi, h, t: (bi, t, h)),
                   pl.BlockSpec((None, None, HGRN_EXPAND, HGRN_EXPAND), lambda bi, h, t: (bi, h, 0, 0))],
        out_shape=[jax.ShapeDtypeStruct((b, l, HGRN_WIDTH), F32),
                   jax.ShapeDtypeStruct((b, nh, HGRN_EXPAND, HGRN_EXPAND), F32)],
        scratch_shapes=[pltpu.VMEM((HGRN_EXPAND, HGRN_EXPAND), F32)],
        compiler_params=pltpu.CompilerParams(
            dimension_semantics=("arbitrary", "arbitrary", "arbitrary"), vmem_limit_bytes=VMEM_LIMIT_BYTES),
        name="hgrn_mix",
    )(proj, proj, proj, proj, lb.reshape(1, HGRN_WIDTH), norm_w.reshape(1, HGRN_EXPAND), s0)


SSD_TIME_TILE = 128
SSD_TAIL_ROW = SUBLANES - (CONV_WIDTH - 1)
SSD_HEAD_PAIR = 2 * SSM_HEAD_DIM


def _ssd_kernel(z_ref, x_ref, bc_ref, dt_ref, cbuf_ref, s0_ref, cw_ref, cb_ref, dtb_ref, alog_ref,
                dskip_ref, nw_ref, y_ref, nbuf_ref, sfin_ref, xpad_ref, st_ref):
    tl = SSD_TIME_TILE
    gn = SSM_GROUPS * SSM_STATE
    t_idx = pl.program_id(1)

    @pl.when(t_idx == 0)
    def _():
        xpad_ref[SSD_TAIL_ROW:SUBLANES, :] = cbuf_ref[...]
        st_ref[...] = s0_ref[...].reshape(SSM_INNER, SSM_STATE).T

    xpad_ref[SUBLANES:SUBLANES + tl, 0:SSM_INNER] = x_ref[...]
    xpad_ref[SUBLANES:SUBLANES + tl, SSM_INNER:CONV_DIM] = bc_ref[...]
    conv = cb_ref[...] + xpad_ref[SSD_TAIL_ROW:SSD_TAIL_ROW + tl, :] * cw_ref[0:1, :]
    for j in range(1, CONV_WIDTH):
        conv = conv + xpad_ref[SSD_TAIL_ROW + j:SSD_TAIL_ROW + j + tl, :] * cw_ref[j:j + 1, :]
    tail = xpad_ref[tl + SSD_TAIL_ROW:tl + SUBLANES, :]
    xpad_ref[SSD_TAIL_ROW:SUBLANES, :] = tail
    nbuf_ref[...] = tail
    act = conv * jax.nn.sigmoid(conv)
    xs = act[:, 0:SSM_INNER]
    bm = act[:, SSM_INNER:SSM_INNER + gn]
    cm = act[:, SSM_INNER + gn:CONV_DIM]

    dtv = dt_ref[...] + dtb_ref[...]
    dt = jnp.maximum(dtv, 0.0) + jnp.log(1.0 + jnp.exp(-jnp.abs(dtv)))
    acum = _block_cumsum(dt * (-jnp.exp(alog_ref[...])), tl)
    acum_t = acum.T
    dt_t = dt.T
    a_end = acum[tl - 1:tl, :]
    wdt = jnp.exp(a_end - acum) * dt
    e_end = jnp.exp(a_end)
    tri = (lax.broadcasted_iota(jnp.int32, (tl, tl), 1) <= lax.broadcasted_iota(jnp.int32, (tl, tl), 0))
    lo = lax.broadcasted_iota(jnp.int32, (1, SSD_HEAD_PAIR), 1) < SSM_HEAD_DIM

    for g in range(SSM_GROUPS):
        cm_g = cm[:, g * SSM_STATE:(g + 1) * SSM_STATE]
        bm_g = bm[:, g * SSM_STATE:(g + 1) * SSM_STATE]
        cb = lax.dot_general(cm_g.astype(BF16), bm_g.astype(BF16), (((1,), (1,)), ((), ())),
                             preferred_element_type=F32)
        for pp in range(SSM_HPG // 2):
            h0 = g * SSM_HPG + 2 * pp
            cols = slice(h0 * SSM_HEAD_DIM, h0 * SSM_HEAD_DIM + SSD_HEAD_PAIR)
            xs_pair = xs[:, cols]
            st_pair = st_ref[:, cols]
            y_pair = xs_pair * dskip_ref[:, cols]
            upd = jnp.zeros((SSM_STATE, SSD_HEAD_PAIR), F32)
            for h, keep in ((h0, lo), (h0 + 1, jnp.logical_not(lo))):
                col_a = jnp.broadcast_to(acum[:, h:h + 1], (tl, tl))
                decay = jnp.where(tri, jnp.exp(jnp.where(tri, col_a - acum_t[h:h + 1, :], 0.0)), 0.0)
                m = cb * decay * dt_t[h:h + 1, :]
                cme = cm_g * jnp.exp(col_a)
                lhs = jnp.concatenate([m, cme], axis=1).astype(BF16)
                xk = jnp.where(keep, xs_pair, 0.0)
                rhs = jnp.concatenate([xk, jnp.where(keep, st_pair, 0.0)], axis=0).astype(BF16)
                y_pair = y_pair + jnp.dot(lhs, rhs, preferred_element_type=F32)
                bms = (bm_g * jnp.broadcast_to(wdt[:, h:h + 1], (tl, SSM_STATE))).astype(BF16)
                upd = upd + lax.dot_general(bms, xk.astype(BF16), (((0,), (0,)), ((), ())),
                                            preferred_element_type=F32)
            dec = jnp.where(lo, jnp.broadcast_to(e_end[:, h0:h0 + 1], (1, SSD_HEAD_PAIR)),
                            jnp.broadcast_to(e_end[:, h0 + 1:h0 + 2], (1, SSD_HEAD_PAIR)))
            st_ref[:, cols] = st_pair * dec + upd
            y_ref[:, cols] = y_pair

    z = z_ref[...]
    y = y_ref[...] * (z * jax.nn.sigmoid(z))
    gw = SSM_INNER // SSM_GROUPS
    for g in range(SSM_GROUPS):
        yg = y[:, g * gw:(g + 1) * gw]
        y_ref[:, g * gw:(g + 1) * gw] = (yg * lax.rsqrt(jnp.mean(yg * yg, axis=-1, keepdims=True) + RMS_EPS)
                                        * nw_ref[:, g * gw:(g + 1) * gw])

    @pl.when(t_idx == pl.num_programs(1) - 1)
    def _():
        sfin_ref[...] = st_ref[...].T.reshape(SSM_HEADS, SSM_HEAD_DIM, SSM_STATE)


def ssd_mix(proj, conv_buf, s0, conv_w, conv_b, dt_bias, a_log, d_skip, norm_w):
    b, l, _ = proj.shape
    tl = SSD_TIME_TILE
    pad = LANES - SSM_HEADS
    row = lambda v, n: v.reshape(1, n)
    bc_w = CONV_DIM - SSM_INNER
    const = lambda shape: pl.BlockSpec(shape, lambda bi, t: (0,) * len(shape))
    return pl.pallas_call(
        _ssd_kernel,
        grid=(b, l // tl),
        in_specs=[pl.BlockSpec((None, tl, SSM_INNER), lambda bi, t: (bi, t, 0)),
                  pl.BlockSpec((None, tl, SSM_INNER), lambda bi, t: (bi, t, 1)),
                  pl.BlockSpec((None, tl, bc_w), lambda bi, t: (bi, t, 2 * SSM_INNER // bc_w)),
                  pl.BlockSpec((None, tl, LANES), lambda bi, t: (bi, t, (SSM_INNER + CONV_DIM) // LANES)),
                  pl.BlockSpec((None, CONV_WIDTH - 1, CONV_DIM), lambda bi, t: (bi, 0, 0)),
                  pl.BlockSpec((None, SSM_HEADS, SSM_HEAD_DIM, SSM_STATE), lambda bi, t: (bi, 0, 0, 0)),
                  const((CONV_WIDTH, CONV_DIM)), const((1, CONV_DIM)), const((1, LANES)), const((1, LANES)),
                  const((1, SSM_INNER)), const((1, SSM_INNER))],
        out_specs=[pl.BlockSpec((None, tl, SSM_INNER), lambda bi, t: (bi, t, 0)),
                   pl.BlockSpec((None, CONV_WIDTH - 1, CONV_DIM), lambda bi, t: (bi, 0, 0)),
                   pl.BlockSpec((None, SSM_HEADS, SSM_HEAD_DIM, SSM_STATE), lambda bi, t: (bi, 0, 0, 0))],
        out_shape=[jax.ShapeDtypeStruct((b, l, SSM_INNER), F32),
                   jax.ShapeDtypeStruct((b, CONV_WIDTH - 1, CONV_DIM), F32),
                   jax.ShapeDtypeStruct((b, SSM_HEADS, SSM_HEAD_DIM, SSM_STATE), F32)],
        scratch_shapes=[pltpu.VMEM((tl + SUBLANES, CONV_DIM), F32),
                        pltpu.VMEM((SSM_STATE, SSM_INNER), F32)],
        compiler_params=pltpu.CompilerParams(
            dimension_semantics=("arbitrary", "arbitrary"), vmem_limit_bytes=VMEM_LIMIT_BYTES),
        name="ssd_mix",
    )(proj, proj, proj, proj, conv_buf, s0, conv_w, row(conv_b, CONV_DIM),
      row(jnp.pad(dt_bias, (0, pad)), LANES), row(jnp.pad(a_log, (0, pad)), LANES),
      row(jnp.repeat(d_skip, SSM_HEAD_DIM), SSM_INNER), row(norm_w, SSM_INNER))


def _to_chunks(a, c):
    b, l = a.shape[:2]
    return jnp.swapaxes(a.reshape((b, l // c, c) + a.shape[2:]), 0, 1)


def _from_chunks(a):
    n, b, c = a.shape[:3]
    return jnp.swapaxes(a, 0, 1).reshape((b, n * c) + a.shape[3:])


def _masked_decay(diff, mask):
    return jnp.where(mask, jnp.exp(jnp.where(mask, diff, 0.0)), 0.0)


def _rwkv7_mixer(p, shift_prev, s0, mu, w0, w_up, a0, a_up, g_up, k_k, k_a, r_k, gn_w, gn_b):
    b, l, _ = p.shape
    prev = jnp.concatenate([shift_prev[:, None], p[:, :-1]], axis=1)
    m = p + mu * (prev - p)
    r, k, v, wd, ad, gd = jnp.split(m, RWKV_SPLITS, axis=-1)
    w_raw = w0 + jnp.tanh(wd) @ w_up
    decay = jnp.exp(-jnp.exp(-jax.nn.softplus(-w_raw) - 0.5))
    a = jax.nn.sigmoid(a0 + ad @ a_up)
    g = jax.nn.sigmoid(gd) @ g_up
    heads = lambda t: t.reshape(b, l, RWKV_HEADS, RWKV_HEAD_DIM)
    kk = heads(k * k_k)
    kk = kk / jnp.maximum(jnp.sqrt(jnp.sum(kk * kk, axis=-1, keepdims=True)), L2_EPS)
    k = heads(k * (1.0 + (a - 1.0) * k_a))
    r, v, a, decay = heads(r), heads(v), heads(a), heads(decay)

    o, s_fin = rwkv_scan(r, decay, k, v, kk, a, s0)
    mean = jnp.mean(o, axis=-1, keepdims=True)
    var = jnp.mean(jnp.square(o - mean), axis=-1, keepdims=True)
    o = ((o - mean) * lax.rsqrt(var + GN_EPS)).reshape(b, l, RWKV_WIDTH) * gn_w + gn_b
    bonus = jnp.sum(r * k * r_k, axis=-1, keepdims=True) * v
    out = (o + bonus.reshape(b, l, RWKV_WIDTH)) * g
    return out, p[:, -1], s_fin


def _chunked_gla(q, k, v, g, s0):
    c = math.gcd(q.shape[1], CHUNK)
    mask = jnp.tril(jnp.ones((c, c), dtype=bool))[None, :, :, None, None]

    def step(s, inp):
        qc, kc, vc, gc = inp
        gcum = jnp.cumsum(gc, axis=1)
        rel = _masked_decay(gcum[:, :, None] - gcum[:, None, :], mask)
        att = jnp.einsum('bthk,btshk->btsh', qc, rel * kc[:, None])
        o = (jnp.einsum('btsh,bshv->bthv', att, vc)
             + jnp.einsum('bthk,bhkv->bthv', qc * jnp.exp(gcum), s))
        g_end = gcum[:, -1]
        s = s * jnp.exp(g_end)[..., None] + jnp.einsum(
            'bshk,bshv->bhkv', kc * jnp.exp(g_end[:, None] - gcum), vc)
        return s, o

    s_fin, o = lax.scan(step, s0, tuple(_to_chunks(t, c) for t in (q, k, v, g)))
    return _from_chunks(o), s_fin


def _hgrn2_mixer(p, s0, lb, norm_w):
    b, l, _ = p.shape
    q, f, i, og = jnp.split(p, 4, axis=-1)
    fg = lb + (1.0 - lb) * jax.nn.sigmoid(f)
    log_f = jnp.log(fg)
    k = 1.0 - fg
    heads = lambda t: t.reshape(b, l, HGRN_HEADS, HGRN_EXPAND)
    o, s_fin = _chunked_gla(heads(jax.nn.silu(q)), heads(k), heads(i), heads(log_f), s0)
    o = o * lax.rsqrt(jnp.mean(o * o, axis=-1, keepdims=True) + RMS_EPS) * norm_w
    out = o.reshape(b, l, HGRN_WIDTH) * jax.nn.silu(og)
    return out, s_fin


def _chunked_ssd(x, bm, cm, log_a, s0):
    c = math.gcd(x.shape[1], CHUNK)
    mask = jnp.tril(jnp.ones((c, c), dtype=bool))[None, :, :, None, None]

    def step(s, inp):
        xc, bc, cc, ac = inp
        acum = jnp.cumsum(ac, axis=1)
        lmat = _masked_decay(acum[:, :, None] - acum[:, None, :], mask)
        cb = jnp.einsum('btgn,bsgn->btsg', cc, bc)
        y = jnp.einsum('btsgr,bsgrp->btgrp', cb[..., None] * lmat, xc)
        y = y + jnp.einsum('btgn,bgrpn->btgrp', cc, s) * jnp.exp(acum)[..., None]
        a_end = acum[:, -1]
        s = s * jnp.exp(a_end)[..., None, None] + jnp.einsum(
            'bsgn,bsgrp->bgrpn', bc, xc * jnp.exp(a_end[:, None] - acum)[..., None])
        return s, y

    s_fin, y = lax.scan(step, s0, tuple(_to_chunks(t, c) for t in (x, bm, cm, log_a)))
    return _from_chunks(y), s_fin


def _mamba2_mixer(p, conv_buf, s0, conv_w, conv_b, dt_bias, a_log, d_skip, norm_w):
    b, l, _ = p.shape
    z, xbc, dt = jnp.split(p, [SSM_INNER, SSM_INNER + CONV_DIM], axis=-1)
    xpad = jnp.concatenate([conv_buf, xbc], axis=1)
    conv = conv_b + xpad[:, 0:l] * conv_w[0]
    for j in range(1, CONV_WIDTH):
        conv = conv + xpad[:, j:j + l] * conv_w[j]
    xbc = jax.nn.silu(conv)
    new_buf = xpad[:, -(CONV_WIDTH - 1):]
    xs, bm, cm = jnp.split(xbc, [SSM_INNER, SSM_INNER + SSM_GROUPS * SSM_STATE], axis=-1)
    xs = xs.reshape(b, l, SSM_GROUPS, SSM_HPG, SSM_HEAD_DIM)
    bm = bm.reshape(b, l, SSM_GROUPS, SSM_STATE)
    cm = cm.reshape(b, l, SSM_GROUPS, SSM_STATE)
    dt = jax.nn.softplus(dt + dt_bias).reshape(b, l, SSM_GROUPS, SSM_HPG)
    log_a = dt * (-jnp.exp(a_log)).reshape(SSM_GROUPS, SSM_HPG)
    s0 = s0.reshape(b, SSM_GROUPS, SSM_HPG, SSM_HEAD_DIM, SSM_STATE)
    y, s_fin = _chunked_ssd(xs * dt[..., None], bm, cm, log_a, s0)
    y = y + xs * d_skip.reshape(SSM_GROUPS, SSM_HPG, 1)
    y = y.reshape(b, l, SSM_INNER) * jax.nn.silu(z)
    yg = y.reshape(b, l, SSM_GROUPS, SSM_INNER // SSM_GROUPS)
    yg = yg * lax.rsqrt(jnp.mean(yg * yg, axis=-1, keepdims=True) + RMS_EPS)
    out = yg.reshape(b, l, SSM_INNER) * norm_w
    return out, new_buf, s_fin.reshape(b, SSM_HEADS, SSM_HEAD_DIM, SSM_STATE)


def _prepare_weights(w):
    ssm_pad = (-SSM_PROJ) % (6 * LANES)
    return dict(
        ab_w_in=w['ab_w_in'].astype(BF16),
        ab_w_out=w['ab_w_out'].astype(BF16),
        ssm_w_in=jnp.pad(w['ssm_w_in'], ((0, 0), (0, 0), (0, ssm_pad))).astype(BF16),
        ssm_w_out=w['ssm_w_out'].astype(BF16),
        peer_wqt=jnp.swapaxes(w['peer_w_q'], 1, 2).astype(BF16),
        peer_keys=w['peer_sub_keys'].reshape(DEPTH, 2 * PEER_HEADS, PEER_KEYS, PEER_HALF).astype(BF16),
        peer_u=w['peer_u'].astype(BF16),
        peer_vt=jnp.swapaxes(w['peer_v'], 1, 2).astype(BF16),
    )


def _run_trunk(x, st_shift, st_rwkv, st_hgrn, st_ssm, st_conv, w, wp):
    b, l, d = x.shape
    t = b * l
    lb = jax.nn.softmax(w['hgrn_lower_bounds'], axis=0)
    lb = jnp.cumsum(lb, axis=0) - lb[0]
    n_shift, n_rwkv, n_hgrn, n_ssm, n_conv = [], [], [], [], []
    xt = x.reshape(t, d)
    for i in range(DEPTH):
        j = i // 2
        if i % 2 == 0:
            proj = norm_matmul(xt, w['norm_mix'][i], wp['ab_w_in'][j]).reshape(b, l, AB_PROJ)
            o_a, sh, sa = _rwkv7_mixer(proj[..., :RWKV_PROJ], st_shift[j], st_rwkv[j],
                                       w['rwkv_mu'][j], w['rwkv_w0'][j], w['rwkv_w_up'][j], w['rwkv_a0'][j],
                                       w['rwkv_a_up'][j], w['rwkv_g_up'][j], w['rwkv_k_k'][j], w['rwkv_k_a'][j],
                                       w['rwkv_r_k'][j], w['rwkv_gn_w'][j], w['rwkv_gn_b'][j])
            if l % HGRN_TIME_TILE == 0:
                o_b, sb = hgrn_mix(proj, st_hgrn[j], lb[j], w['hgrn_norm_w'][j])
            else:
                o_b, sb = _hgrn2_mixer(proj[..., RWKV_PROJ:], st_hgrn[j], lb[j], w['hgrn_norm_w'][j])
            mix_in = jnp.concatenate([o_a, o_b], axis=-1).reshape(t, MIX_WIDTH)
            xt = matmul_residual(mix_in, wp['ab_w_out'][j], xt)
            n_shift.append(sh); n_rwkv.append(sa); n_hgrn.append(sb)
        else:
            proj = norm_matmul(xt, w['norm_mix'][i], wp['ssm_w_in'][j])
            ssm_w = (w['ssm_conv_w'][j], w['ssm_conv_b'][j], w['ssm_dt_bias'][j], w['ssm_a_log'][j],
                     w['ssm_d'][j], w['ssm_norm_w'][j])
            if l % SSD_TIME_TILE == 0:
                o_c, cb, sc = ssd_mix(proj.reshape(b, l, -1), st_conv[j], st_ssm[j], *ssm_w)
            else:
                o_c, cb, sc = _mamba2_mixer(proj[:, :SSM_PROJ].reshape(b, l, SSM_PROJ), st_conv[j], st_ssm[j], *ssm_w)
            xt = matmul_residual(o_c.reshape(t, SSM_INNER), wp['ssm_w_out'][j], xt)
            n_conv.append(cb); n_ssm.append(sc)
        xt = peer_block(xt, w['norm_ffn'][i], wp['peer_wqt'][i], wp['peer_keys'][i],
                        wp['peer_u'][i], wp['peer_vt'][i])
    y = rms_norm_rows(xt, w['norm_final']).reshape(b, l, d)
    return (y, jnp.stack(n_shift), jnp.stack(n_rwkv), jnp.stack(n_hgrn), jnp.stack(n_ssm), jnp.stack(n_conv))


def kernel(x_prompt, x_sample, state_rwkv_shift, state_rwkv, state_hgrn, state_ssm, state_conv,
           norm_mix, norm_ffn, norm_final, ab_w_in, ab_w_out, rwkv_mu, rwkv_w0, rwkv_w_up, rwkv_a0,
           rwkv_a_up, rwkv_g_up, rwkv_k_k, rwkv_k_a, rwkv_r_k, rwkv_gn_w, rwkv_gn_b, hgrn_lower_bounds,
           hgrn_norm_w, ssm_w_in, ssm_conv_w, ssm_conv_b, ssm_dt_bias, ssm_a_log, ssm_d, ssm_norm_w,
           ssm_w_out, peer_w_q, peer_sub_keys, peer_u, peer_v):
    w = dict(norm_mix=norm_mix, norm_ffn=norm_ffn, norm_final=norm_final, ab_w_in=ab_w_in, ab_w_out=ab_w_out,
             rwkv_mu=rwkv_mu, rwkv_w0=rwkv_w0, rwkv_w_up=rwkv_w_up, rwkv_a0=rwkv_a0, rwkv_a_up=rwkv_a_up,
             rwkv_g_up=rwkv_g_up, rwkv_k_k=rwkv_k_k, rwkv_k_a=rwkv_k_a, rwkv_r_k=rwkv_r_k, rwkv_gn_w=rwkv_gn_w,
             rwkv_gn_b=rwkv_gn_b, hgrn_lower_bounds=hgrn_lower_bounds, hgrn_norm_w=hgrn_norm_w,
             ssm_w_in=ssm_w_in, ssm_conv_w=ssm_conv_w, ssm_conv_b=ssm_conv_b, ssm_dt_bias=ssm_dt_bias,
             ssm_a_log=ssm_a_log, ssm_d=ssm_d, ssm_norm_w=ssm_norm_w, ssm_w_out=ssm_w_out,
             peer_w_q=peer_w_q, peer_sub_keys=peer_sub_keys, peer_u=peer_u, peer_v=peer_v)
    wp = _prepare_weights(w)
    bp = x_prompt.shape[0]
    prompt = _run_trunk(
        x_prompt,
        jnp.zeros((DEPTH // 2, bp, RWKV_PROJ), F32),
        jnp.zeros((DEPTH // 2, bp, RWKV_HEADS, RWKV_HEAD_DIM, RWKV_HEAD_DIM), F32),
        jnp.zeros((DEPTH // 2, bp, HGRN_HEADS, HGRN_EXPAND, HGRN_EXPAND), F32),
        jnp.zeros((DEPTH // 2, bp, SSM_HEADS, SSM_HEAD_DIM, SSM_STATE), F32),
        jnp.zeros((DEPTH // 2, bp, CONV_WIDTH - 1, CONV_DIM), F32),
        w, wp)
    sample = _run_trunk(x_sample, state_rwkv_shift, state_rwkv, state_hgrn, state_ssm, state_conv, w, wp)
    return (prompt[0], sample[0]) + prompt[1:] + sample[1:]
```
